```python
import jax, jax.numpy as jnp
from jax import lax
import numpy as np

D_MODEL = 1024
BATCH = 8
SEQ = 2048
DEPTH = 4

N_HEADS = 8
HEAD_DIM = 64
ATTN_WIDTH = N_HEADS * HEAD_DIM
MOBA_BLOCK = 256
MOBA_TOPK = 3
Q_CHUNK = 128
POOL_WINDOWS = (2, 4, 8, 16)
N_POOL_GROUPS = len(POOL_WINDOWS)
POOL_WIDTH = D_MODEL // 2
POOL_GROUP_IN = POOL_WIDTH // N_POOL_GROUPS
POOL_GROUP_OUT = D_MODEL // N_POOL_GROUPS
N_BRANCHES = 2
IN_PROJ_WIDTH = 3 * ATTN_WIDTH + POOL_WIDTH + N_BRANCHES * D_MODEL
N_EXPERT_GROUPS = 4
EXPERTS_PER_GROUP = 8
N_EXPERTS = N_EXPERT_GROUPS * EXPERTS_PER_GROUP
TOP_K_INNER = 2
EXPERT_HIDDEN = 256
N_ADA = 6
EPS = 1e-6

kernel_name = "hybrid_moba_pool_hiermoe_adaln"


def rmsnorm(x, g):
    xf = x.astype(jnp.float32)
    xf = xf * lax.rsqrt(jnp.mean(xf * xf, axis=-1, keepdims=True) + EPS)
    return (xf * g.astype(jnp.float32)).astype(x.dtype)


def modulate(h, shift, scale):
    return h * (1.0 + scale[:, None, :]) + shift[:, None, :]


def moba_attention(q, k, v):
    B, H, S, Dh = q.shape
    nb = -(-S // MOBA_BLOCK)
    pad = nb * MOBA_BLOCK - S
    k_pad = jnp.pad(k, ((0, 0), (0, 0), (0, pad), (0, 0)))
    v_pad = jnp.pad(v, ((0, 0), (0, 0), (0, pad), (0, 0)))
    k_blocks = k_pad.reshape(B, H, nb, MOBA_BLOCK, Dh)
    v_blocks = v_pad.reshape(B, H, nb, MOBA_BLOCK, Dh)
    k_mean = jnp.mean(k_blocks, axis=3)
    n_sel = min(MOBA_TOPK, max(nb - 1, 1))
    scale = HEAD_DIM ** -0.5
    n_chunks = S // Q_CHUNK
    b_ix = jnp.arange(B)[:, None, None, None]
    h_ix = jnp.arange(H)[None, :, None, None]
    block_ids = jnp.arange(nb)

    def one_chunk(ci):
        q0 = ci * Q_CHUNK
        qc = lax.dynamic_slice_in_dim(q, q0, Q_CHUNK, axis=2)
        own = q0 // MOBA_BLOCK
        q_pos = q0 + jnp.arange(Q_CHUNK)
        gate = jnp.einsum('bhqd,bhnd->bhqn', qc, k_mean).astype(jnp.float32)
        gate = jnp.where((block_ids < own)[None, None, None, :], gate, -jnp.inf)
        g_val, idx = lax.top_k(gate, n_sel)
        valid = jnp.isfinite(g_val)
        k_g = k_blocks[b_ix, h_ix, idx]
        v_g = v_blocks[b_ix, h_ix, idx]
        s_sel = jnp.einsum('bhqd,bhqksd->bhqks', qc, k_g).astype(jnp.float32) * scale
        s_sel = jnp.where(valid[..., None], s_sel, -jnp.inf)
        k_own = lax.dynamic_slice_in_dim(k_pad, own * MOBA_BLOCK, MOBA_BLOCK, axis=2)
        v_own = lax.dynamic_slice_in_dim(v_pad, own * MOBA_BLOCK, MOBA_BLOCK, axis=2)
        k_pos = own * MOBA_BLOCK + jnp.arange(MOBA_BLOCK)
        s_own = jnp.einsum('bhqd,bhsd->bhqs', qc, k_own).astype(jnp.float32) * scale
        s_own = jnp.where((k_pos[None, :] <= q_pos[:, None])[None, None], s_own, -jnp.inf)
        s_all = jnp.concatenate([s_sel.reshape(B, H, Q_CHUNK, n_sel * MOBA_BLOCK), s_own], axis=-1)
        p = jax.nn.softmax(s_all, axis=-1)
        p_sel = p[..., :n_sel * MOBA_BLOCK].reshape(B, H, Q_CHUNK, n_sel, MOBA_BLOCK).astype(v.dtype)
        p_own = p[..., n_sel * MOBA_BLOCK:].astype(v.dtype)
        return (jnp.einsum('bhqks,bhqksd->bhqd', p_sel, v_g)
                + jnp.einsum('bhqs,bhsd->bhqd', p_own, v_own))

    out = lax.map(one_chunk, jnp.arange(n_chunks))
    return out.transpose(1, 2, 0, 3, 4).reshape(B, H, S, Dh)


def pool_branch(p, w_pool, pool_scale):
    B, S, _ = p.shape
    pf = p.astype(jnp.float32).reshape(B, S, N_POOL_GROUPS, POOL_GROUP_IN)
    pos = jnp.arange(S)
    outs = []
    for gi, w in enumerate(POOL_WINDOWS):
        xg = pf[:, :, gi]
        cs = jnp.cumsum(xg, axis=1)
        lag = jnp.pad(cs, ((0, 0), (w, 0), (0, 0)))[:, :S]
        cnt = jnp.minimum(pos + 1, w).astype(jnp.float32)[None, :, None]
        outs.append((cs - lag) / cnt - xg)
    pooled = jnp.stack(outs, axis=2).astype(p.dtype)
    y = jnp.einsum('bsgc,gco->bsgo', pooled, w_pool).reshape(B, S, D_MODEL)
    return y * pool_scale


def hier_moe(h, w_router_g, b_router_g, w_router_e, b_router_e, w1, w3, w2):
    B, S, D = h.shape
    T = B * S
    t = h.reshape(T, D)
    g_logits = (t @ w_router_g).astype(jnp.float32) + b_router_g
    g_prob = jax.nn.softmax(g_logits, axis=-1)
    g_top, g_idx = lax.top_k(g_prob, 1)
    e_logits = ((t @ w_router_e).astype(jnp.float32) + b_router_e).reshape(T, N_EXPERT_GROUPS, EXPERTS_PER_GROUP)
    e_in = jnp.take_along_axis(e_logits, g_idx[:, :, None], axis=1)[:, 0]
    e_top, e_idx = lax.top_k(e_in, TOP_K_INNER)
    e_w = jax.nn.softmax(e_top, axis=-1) * g_top
    expert_id = g_idx * EXPERTS_PER_GROUP + e_idx
    combine = jnp.einsum('tk,tke->te', e_w, jax.nn.one_hot(expert_id, N_EXPERTS, dtype=jnp.float32))
    combine = combine.reshape(T, N_EXPERT_GROUPS, EXPERTS_PER_GROUP).astype(h.dtype)
    out = jnp.zeros((T, D), dtype=h.dtype)
    for g in range(N_EXPERT_GROUPS):
        a = jnp.einsum('td,edf->tef', t, w1[g])
        b = jnp.einsum('td,edf->tef', t, w3[g])
        hid = jax.nn.silu(a) * b * combine[:, g, :, None]
        out = out + jnp.einsum('tef,efd->td', hid, w2[g])
    return out.reshape(B, S, D)


def setup_inputs(seed: int = 0) -> dict:
    key = jax.random.key(seed)
    ks = jax.random.split(key, 24)
    f32 = jnp.float32
    L, D = DEPTH, D_MODEL

    def nrm(k, shape, s):
        return jax.random.normal(k, shape, f32) * s

    return {
        "x": nrm(ks[0], (BATCH, SEQ, D), 1.0),
        "c": nrm(ks[1], (BATCH, D), 1.0),
        "w_ada": nrm(ks[2], (L, D, N_ADA * D), 0.5 * D ** -0.5),
        "b_ada": nrm(ks[3], (L, N_ADA * D), 0.01),
        "norm1": 1.0 + nrm(ks[4], (L, D), 0.05),
        "w_in": nrm(ks[5], (L, D, IN_PROJ_WIDTH), D ** -0.5),
        "q_norm": 1.0 + nrm(ks[6], (L, HEAD_DIM), 0.05),
        "k_norm": 1.0 + nrm(ks[7], (L, HEAD_DIM), 0.05),
        "w_attn_up": nrm(ks[8], (L, ATTN_WIDTH, D), ATTN_WIDTH ** -0.5),
        "w_pool": nrm(ks[9], (L, N_POOL_GROUPS, POOL_GROUP_IN, POOL_GROUP_OUT), POOL_GROUP_IN ** -0.5),
        "pool_scale": 1.0 + nrm(ks[10], (L, D), 0.1),
        "w_out": nrm(ks[11], (L, D, D), D ** -0.5),
        "norm2": 1.0 + nrm(ks[12], (L, D), 0.05),
        "w_router_g": nrm(ks[13], (L, D, N_EXPERT_GROUPS), D ** -0.5),
        "b_router_g": nrm(ks[14], (L, N_EXPERT_GROUPS), 0.01),
        "w_router_e": nrm(ks[15], (L, D, N_EXPERTS), D ** -0.5),
        "b_router_e": nrm(ks[16], (L, N_EXPERTS), 0.01),
        "w1": nrm(ks[17], (L, N_EXPERT_GROUPS, EXPERTS_PER_GROUP, D, EXPERT_HIDDEN), D ** -0.5),
        "w3": nrm(ks[18], (L, N_EXPERT_GROUPS, EXPERTS_PER_GROUP, D, EXPERT_HIDDEN), D ** -0.5),
        "w2": nrm(ks[19], (L, N_EXPERT_GROUPS, EXPERTS_PER_GROUP, EXPERT_HIDDEN, D), EXPERT_HIDDEN ** -0.5),
    }


def reference(x, c, w_ada, b_ada, norm1, w_in, q_norm, k_norm, w_attn_up, w_pool, pool_scale,
              w_out, norm2, w_router_g, b_router_g, w_router_e, b_router_e, w1, w3, w2):
    B, S, D = x.shape
    c_act = jax.nn.silu(c)
    splits = [ATTN_WIDTH, 2 * ATTN_WIDTH, 3 * ATTN_WIDTH, 3 * ATTN_WIDTH + POOL_WIDTH,
              3 * ATTN_WIDTH + POOL_WIDTH + D_MODEL]
    for l in range(DEPTH):
        mod = c_act @ w_ada[l] + b_ada[l]
        sh1, sc1, g1, sh2, sc2, g2 = jnp.split(mod, N_ADA, axis=-1)
        h = modulate(rmsnorm(x, norm1[l]), sh1, sc1)
        z = h @ w_in[l]
        q, k, v, p_in, ga, gp = jnp.split(z, splits, axis=-1)
        q = rmsnorm(q.reshape(B, S, N_HEADS, HEAD_DIM), q_norm[l]).transpose(0, 2, 1, 3)
        k = rmsnorm(k.reshape(B, S, N_HEADS, HEAD_DIM), k_norm[l]).transpose(0, 2, 1, 3)
        v = v.reshape(B, S, N_HEADS, HEAD_DIM).transpose(0, 2, 1, 3)
        attn = moba_attention(q, k, v).transpose(0, 2, 1, 3).reshape(B, S, ATTN_WIDTH)
        branch_a = attn @ w_attn_up[l]
        branch_p = pool_branch(p_in, w_pool[l], pool_scale[l])
        merged = jax.nn.sigmoid(ga) * branch_a + jax.nn.sigmoid(gp) * branch_p
        x = x + g1[:, None, :] * (merged @ w_out[l])
        h = modulate(rmsnorm(x, norm2[l]), sh2, sc2)
        x = x + g2[:, None, :] * hier_moe(h, w_router_g[l], b_router_g[l], w_router_e[l], b_router_e[l],
                                          w1[l], w3[l], w2[l])
    return x
```

```python
import functools

import jax
import jax.numpy as jnp
from jax import lax
from jax.experimental import pallas as pl
from jax.experimental.pallas import tpu as pltpu

F32 = jnp.float32
BF16 = jnp.bfloat16

D_MODEL = 1024
N_HEADS = 8
HEAD_DIM = 64
ATTN_WIDTH = N_HEADS * HEAD_DIM
MOBA_BLOCK = 256
MOBA_TOPK = 3
POOL_WINDOWS = (2, 4, 8, 16)
N_POOL_GROUPS = len(POOL_WINDOWS)
POOL_WIDTH = D_MODEL // 2
POOL_GROUP_IN = POOL_WIDTH // N_POOL_GROUPS
POOL_GROUP_OUT = D_MODEL // N_POOL_GROUPS
N_EXPERT_GROUPS = 4
EXPERTS_PER_GROUP = 8
N_EXPERTS = N_EXPERT_GROUPS * EXPERTS_PER_GROUP
EXPERT_HIDDEN = 256
N_ADA = 6
EPS = 1e-6

LANES = 128
POOL_HALO = 16
QKV_WIDTH = 3 * ATTN_WIDTH
GATE_OFF = QKV_WIDTH + POOL_WIDTH
VMEM_LIMIT = 56 * 1024 * 1024

TM_IN = 512
TM_MERGE = 512
TM_MOE = 1024
TN_ADA = 1536


def _params(*sem):
    return pltpu.CompilerParams(dimension_semantics=sem, vmem_limit_bytes=VMEM_LIMIT)


def _ada_kernel(c_ref, w_ref, b_ref, o_ref):
    c = c_ref[...]
    c_act = (c * jax.nn.sigmoid(c)).astype(BF16)
    o_ref[...] = jnp.dot(c_act, w_ref[...].astype(BF16), preferred_element_type=F32) + b_ref[...]


def _ada(c, w_ada, b_ada):
    n_layers, d, n = w_ada.shape
    b = c.shape[0]
    return pl.pallas_call(
        _ada_kernel,
        grid=(n_layers, n // TN_ADA),
        in_specs=[
            pl.BlockSpec((b, d), lambda l, j: (0, 0)),
            pl.BlockSpec((None, d, TN_ADA), lambda l, j: (l, 0, j)),
            pl.BlockSpec((None, 1, TN_ADA), lambda l, j: (l, 0, j)),
        ],
        out_specs=pl.BlockSpec((None, b, TN_ADA), lambda l, j: (l, 0, j)),
        out_shape=jax.ShapeDtypeStruct((n_layers, b, n), F32),
        compiler_params=_params("parallel", "parallel"),
        name="ada",
    )(c, w_ada, b_ada.reshape(n_layers, 1, n))


def _modulated_norm(x, gain, shift, scale):
    ms = jnp.mean(x * x, axis=-1, keepdims=True)
    xn = x * lax.rsqrt(ms + EPS) * gain
    return xn * (1.0 + scale) + shift


def _in_kernel(x_ref, mod_ref, g_ref, w_ref, qkv_ref, pin_ref, gates_ref):
    h = _modulated_norm(x_ref[...], g_ref[...], mod_ref[:, 0:D_MODEL],
                        mod_ref[:, D_MODEL:2 * D_MODEL]).astype(BF16)
    qkv_ref[...] = jnp.dot(h, w_ref[:, 0:QKV_WIDTH], preferred_element_type=F32).astype(BF16)
    pin_ref[...] = jnp.dot(h, w_ref[:, QKV_WIDTH:GATE_OFF], preferred_element_type=F32).astype(BF16)
    gates_ref[...] = jnp.dot(h, w_ref[:, GATE_OFF:], preferred_element_type=F32).astype(BF16)


def _in_proj(x, mod, gain, w_in, seq):
    t, d = x.shape
    n = w_in.shape[1]
    tiles_per_seq = seq // TM_IN
    return pl.pallas_call(
        _in_kernel,
        grid=(t // TM_IN,),
        in_specs=[
            pl.BlockSpec((TM_IN, d), lambda i: (i, 0)),
            pl.BlockSpec((None, 1, N_ADA * d), lambda i: (i // tiles_per_seq, 0, 0)),
            pl.BlockSpec((1, d), lambda i: (0, 0)),
            pl.BlockSpec((d, n), lambda i: (0, 0)),
        ],
        out_specs=[
            pl.BlockSpec((TM_IN, QKV_WIDTH), lambda i: (i, 0)),
            pl.BlockSpec((TM_IN, POOL_WIDTH), lambda i: (i, 0)),
            pl.BlockSpec((TM_IN, 2 * d), lambda i: (i, 0)),
        ],
        out_shape=[
            jax.ShapeDtypeStruct((t, QKV_WIDTH), BF16),
            jax.ShapeDtypeStruct((t, POOL_WIDTH), BF16),
            jax.ShapeDtypeStruct((t, 2 * d), BF16),
        ],
        compiler_params=_params("parallel"),
        name="in_proj",
    )(x, mod, gain, w_in)


def _att_kernel(q_ref, k_ref, v_ref, qg_ref, kg_ref, o_ref, kn_ref, km_ref, *, seq):
    nb = seq // MOBA_BLOCK
    lane = lax.broadcasted_iota(jnp.int32, (1, LANES), 1)
    m_a = (lane < HEAD_DIM).astype(F32)
    m_b = 1.0 - m_a

    def head_norm(z, gain):
        z2 = z * z
        ss_a = jnp.sum(z2 * m_a, axis=-1, keepdims=True)
        ss_b = jnp.sum(z2 * m_b, axis=-1, keepdims=True)
        r = m_a * lax.rsqrt(ss_a * (1.0 / HEAD_DIM) + EPS) + m_b * lax.rsqrt(ss_b * (1.0 / HEAD_DIM) + EPS)
        return z * r * gain

    for j in range(nb):
        rows = pl.ds(j * MOBA_BLOCK, MOBA_BLOCK)
        kn = head_norm(k_ref[rows, :].astype(F32), kg_ref[...])
        kn_ref[rows, :] = kn.astype(BF16)
        km_ref[j:j + 1, :] = jnp.mean(kn, axis=0, keepdims=True)

    blk = lax.broadcasted_iota(jnp.int32, (1, nb), 1)
    row_id = lax.broadcasted_iota(jnp.int32, (MOBA_BLOCK, MOBA_BLOCK), 0)
    col_id = lax.broadcasted_iota(jnp.int32, (MOBA_BLOCK, MOBA_BLOCK), 1)
    causal = col_id <= row_id
    nt = (((1,), (1,)), ((), ()))
    neg_inf = -jnp.inf

    for i in range(nb):
        qrows = pl.ds(i * MOBA_BLOCK, MOBA_BLOCK)
        qn = head_norm(q_ref[qrows, :].astype(F32), qg_ref[...])
        outs = []
        for m_h in (m_a, m_b):
            qh = qn * m_h
            if i > 0:
                gate = lax.dot_general(qh, km_ref[...], nt, precision=lax.Precision.HIGHEST,
                                       preferred_element_type=F32)
                gate = jnp.where(blk < i, gate, neg_inf)
                rank = jnp.zeros(gate.shape, jnp.int32)
                for l in range(i):
                    g_l = gate[:, l:l + 1]
                    ahead = (g_l > gate) | ((g_l == gate) & (l < blk))
                    rank = rank + ahead.astype(jnp.int32)
                chosen = (rank < MOBA_TOPK) & (blk < i)
            qs = (qh * (HEAD_DIM ** -0.5)).astype(BF16)
            s_blocks = []
            m_run = None
            for j in range(i + 1):
                krows = pl.ds(j * MOBA_BLOCK, MOBA_BLOCK)
                s = lax.dot_general(qs, kn_ref[krows, :], nt, preferred_element_type=F32)
                if j < i:
                    s = jnp.where(chosen[:, j:j + 1], s, neg_inf)
                else:
                    s = jnp.where(causal, s, neg_inf)
                s_blocks.append(s)
                m_run = s if m_run is None else jnp.maximum(m_run, s)
            m_row = jnp.max(m_run, axis=-1, keepdims=True)
            l_run = None
            acc = None
            for j in range(i + 1):
                krows = pl.ds(j * MOBA_BLOCK, MOBA_BLOCK)
                p = jnp.exp(s_blocks[j] - m_row)
                l_run = p if l_run is None else l_run + p
                pv = jnp.dot(p.astype(BF16), v_ref[krows, :], preferred_element_type=F32)
                acc = pv if acc is None else acc + pv
            l_row = jnp.sum(l_run, axis=-1, keepdims=True)
            outs.append(acc / l_row)
        o_ref[qrows, :] = (outs[0] * m_a + outs[1] * m_b).astype(o_ref.dtype)


def _attention(qkv, q_gain, k_gain, seq):
    t = qkv.shape[0]
    n_pairs = ATTN_WIDTH // LANES
    kern = functools.partial(_att_kernel, seq=seq)
    return pl.pallas_call(
        kern,
        grid=(t // seq, n_pairs),
        in_specs=[
            pl.BlockSpec((seq, LANES), lambda b, h: (b, h)),
            pl.BlockSpec((seq, LANES), lambda b, h: (b, n_pairs + h)),
            pl.BlockSpec((seq, LANES), lambda b, h: (b, 2 * n_pairs + h)),
            pl.BlockSpec((1, LANES), lambda b, h: (0, 0)),
            pl.BlockSpec((1, LANES), lambda b, h: (0, 0)),
        ],
        out_specs=pl.BlockSpec((seq, LANES), lambda b, h: (b, h)),
        out_shape=jax.ShapeDtypeStruct((t, ATTN_WIDTH), BF16),
        scratch_shapes=[
            pltpu.VMEM((seq, LANES), BF16),
            pltpu.VMEM((seq // MOBA_BLOCK, LANES), F32),
        ],
        compiler_params=_params("parallel", "parallel"),
        name="moba_attention",
    )(qkv, qkv, qkv, q_gain, k_gain)


def _merge_kernel(x_ref, attn_ref, pin_ref, halo_ref, gates_ref, mod_ref, wup_ref, wpool_ref,
                  pscale_ref, wout_ref, g2_ref, wr_ref, br_ref,
                  x1_ref, h2_ref, comb_ref, pool_ref, *, seq):
    tm = x_ref.shape[0]
    d = D_MODEL
    seq_pos = (pl.program_id(0) * tm) % seq

    halo = halo_ref[...].astype(F32)
    pool_ref[0:POOL_HALO, :] = jnp.where(seq_pos == 0, 0.0, halo)
    pool_ref[POOL_HALO:, :] = pin_ref[...].astype(F32)
    pos = seq_pos + lax.broadcasted_iota(jnp.int32, (tm, 1), 0)

    a_up = jnp.dot(attn_ref[...], wup_ref[...], preferred_element_type=F32)
    merged = []
    for g, w in enumerate(POOL_WINDOWS):
        cols = slice(g * POOL_GROUP_IN, (g + 1) * POOL_GROUP_IN)
        tok = pool_ref[POOL_HALO:, cols]
        win = tok
        for k in range(1, w):
            win = win + pool_ref[POOL_HALO - k:POOL_HALO - k + tm, cols]
        cnt = jnp.minimum(pos + 1, w).astype(F32)
        pooled = (win / cnt - tok).astype(BF16)
        ocols = slice(g * POOL_GROUP_OUT, (g + 1) * POOL_GROUP_OUT)
        b_pool = jnp.dot(pooled, wpool_ref[g], preferred_element_type=F32) * pscale_ref[:, ocols]
        ga = gates_ref[:, ocols].astype(F32)
        gp = gates_ref[:, d + g * POOL_GROUP_OUT:d + (g + 1) * POOL_GROUP_OUT].astype(F32)
        merged.append((jax.nn.sigmoid(ga) * a_up[:, ocols] + jax.nn.sigmoid(gp) * b_pool).astype(BF16))
    merged = jnp.concatenate(merged, axis=-1)
    y = jnp.dot(merged, wout_ref[...], preferred_element_type=F32)
    x1 = x_ref[...] + mod_ref[:, 2 * d:3 * d] * y
    x1_ref[...] = x1

    h2 = _modulated_norm(x1, g2_ref[...], mod_ref[:, 3 * d:4 * d], mod_ref[:, 4 * d:5 * d])
    h2_ref[...] = h2.astype(BF16)

    logits = jnp.dot(h2, wr_ref[...], precision=lax.Precision.HIGHEST,
                     preferred_element_type=F32) + br_ref[...]
    lane = lax.broadcasted_iota(jnp.int32, (1, LANES), 1)
    neg_inf = -jnp.inf
    is_group = (lane >= N_EXPERTS) & (lane < N_EXPERTS + N_EXPERT_GROUPS)
    gl = jnp.where(is_group, logits, neg_inf)
    g_max = jnp.max(gl, axis=-1, keepdims=True)
    g_top = 1.0 / jnp.sum(jnp.exp(gl - g_max), axis=-1, keepdims=True)
    g_idx = jnp.min(jnp.where(gl == g_max, lane, LANES), axis=-1, keepdims=True) - N_EXPERTS
    in_group = (lane < N_EXPERTS) & ((lane // EXPERTS_PER_GROUP) == g_idx)
    el = jnp.where(in_group, logits, neg_inf)
    e1 = jnp.max(el, axis=-1, keepdims=True)
    i1 = jnp.min(jnp.where(el == e1, lane, LANES), axis=-1, keepdims=True)
    el2 = jnp.where(lane == i1, neg_inf, el)
    e2 = jnp.max(el2, axis=-1, keepdims=True)
    i2 = jnp.min(jnp.where(el2 == e2, lane, LANES), axis=-1, keepdims=True)
    r = jnp.exp(e2 - e1)
    w_first = g_top / (1.0 + r)
    w_second = w_first * r
    comb_ref[...] = jnp.where(lane == i1, w_first, 0.0) + jnp.where(lane == i2, w_second, 0.0)


def _merge(x, attn, pin, gates, mod, w_up, w_pool, pool_scale, w_out, gain2, w_router, b_router, seq):
    t, d = x.shape
    tm = TM_MERGE
    tiles_per_seq = seq // tm
    halo_per_tile = tm // POOL_HALO
    kern = functools.partial(_merge_kernel, seq=seq)
    const2 = lambda i: (0, 0)
    return pl.pallas_call(
        kern,
        grid=(t // tm,),
        in_specs=[
            pl.BlockSpec((tm, d), lambda i: (i, 0)),
            pl.BlockSpec((tm, ATTN_WIDTH), lambda i: (i, 0)),
            pl.BlockSpec((tm, POOL_WIDTH), lambda i: (i, 0)),
            pl.BlockSpec((POOL_HALO, POOL_WIDTH), lambda i: (jnp.maximum(i * halo_per_tile - 1, 0), 0)),
            pl.BlockSpec((tm, 2 * d), lambda i: (i, 0)),
            pl.BlockSpec((None, 1, N_ADA * d), lambda i: (i // tiles_per_seq, 0, 0)),
            pl.BlockSpec((ATTN_WIDTH, d), const2),
            pl.BlockSpec((N_POOL_GROUPS, POOL_GROUP_IN, POOL_GROUP_OUT), lambda i: (0, 0, 0)),
            pl.BlockSpec((1, d), const2),
            pl.BlockSpec((d, d), const2),
            pl.BlockSpec((1, d), const2),
            pl.BlockSpec((d, LANES), const2),
            pl.BlockSpec((1, LANES), const2),
        ],
        out_specs=[
            pl.BlockSpec((tm, d), lambda i: (i, 0)),
            pl.BlockSpec((tm, d), lambda i: (i, 0)),
            pl.BlockSpec((tm, LANES), lambda i: (i, 0)),
        ],
        out_shape=[
            jax.ShapeDtypeStruct((t, d), F32),
            jax.ShapeDtypeStruct((t, d), BF16),
            jax.ShapeDtypeStruct((t, LANES), F32),
        ],
        scratch_shapes=[pltpu.VMEM((tm + POOL_HALO, POOL_WIDTH), F32)],
        compiler_params=_params("parallel"),
        name="merge_router",
    )(x, attn, pin, pin, gates, mod, w_up, w_pool, pool_scale, w_out, gain2, w_router, b_router)


def _moe_kernel(h_ref, comb_ref, x1_ref, mod_ref, w1_ref, w3_ref, w2_ref, o_ref, acc_ref):
    e = pl.program_id(1)
    d = D_MODEL

    @pl.when(e == 0)
    def _():
        acc_ref[...] = jnp.zeros_like(acc_ref)

    h = h_ref[...]
    a = jnp.dot(h, w1_ref[...], preferred_element_type=F32)
    b = jnp.dot(h, w3_ref[...], preferred_element_type=F32)
    lane = lax.broadcasted_iota(jnp.int32, (1, LANES), 1)
    c = jnp.sum(jnp.where(lane == e, comb_ref[...], 0.0), axis=-1, keepdims=True)
    hid = (a * jax.nn.sigmoid(a) * b * c).astype(BF16)
    acc_ref[...] += jnp.dot(hid, w2_ref[...], preferred_element_type=F32)

    @pl.when(e == pl.num_programs(1) - 1)
    def _():
        o_ref[...] = x1_ref[...] + mod_ref[:, 5 * d:6 * d] * acc_ref[...]


def _moe(h2, comb, x1, mod, w1, w3, w2, seq):
    t, d = x1.shape
    tm = TM_MOE
    tiles_per_seq = seq // tm
    n_exp, _, f = w1.shape
    return pl.pallas_call(
        _moe_kernel,
        grid=(t // tm, n_exp),
        in_specs=[
            pl.BlockSpec((tm, d), lambda i, e: (i, 0)),
            pl.BlockSpec((tm, LANES), lambda i, e: (i, 0)),
            pl.BlockSpec((tm, d), lambda i, e: (i, 0)),
            pl.BlockSpec((None, 1, N_ADA * d), lambda i, e: (i // tiles_per_seq, 0, 0)),
            pl.BlockSpec((None, d, f), lambda i, e: (e, 0, 0)),
            pl.BlockSpec((None, d, f), lambda i, e: (e, 0, 0)),
            pl.BlockSpec((None, f, d), lambda i, e: (e, 0, 0)),
        ],
        out_specs=pl.BlockSpec((tm, d), lambda i, e: (i, 0)),
        out_shape=jax.ShapeDtypeStruct((t, d), F32),
        scratch_shapes=[pltpu.VMEM((tm, d), F32)],
        compiler_params=_params("parallel", "arbitrary"),
        name="experts",
    )(h2, comb, x1, mod, w1, w3, w2)


@jax.jit
def kernel(x, c, w_ada, b_ada, norm1, w_in, q_norm, k_norm, w_attn_up, w_pool, pool_scale, w_out,
           norm2, w_router_g, b_router_g, w_router_e, b_router_e, w1, w3, w2):
    bsz, seq, d = x.shape
    n_layers = w_ada.shape[0]
    t = bsz * seq
    mod_all = _ada(c, w_ada, b_ada).reshape(n_layers, bsz, 1, N_ADA * d)

    pad = LANES - N_EXPERTS - N_EXPERT_GROUPS
    w_router = jnp.concatenate(
        [w_router_e, w_router_g, jnp.zeros((n_layers, d, pad), F32)], axis=-1)
    b_router = jnp.concatenate(
        [b_router_e, b_router_g, jnp.zeros((n_layers, pad), F32)], axis=-1).reshape(n_layers, 1, LANES)
    heads_per_block = LANES // HEAD_DIM
    q_gain = jnp.tile(q_norm, (1, heads_per_block)).reshape(n_layers, 1, LANES)
    k_gain = jnp.tile(k_norm, (1, heads_per_block)).reshape(n_layers, 1, LANES)

    xt = x.reshape(t, d)
    for l in range(n_layers):
        mod = mod_all[l]
        qkv, pin, gates = _in_proj(xt, mod, norm1[l].reshape(1, d), w_in[l].astype(BF16), seq)
        attn = _attention(qkv, q_gain[l], k_gain[l], seq)
        x1, h2, comb = _merge(xt, attn, pin, gates, mod, w_attn_up[l].astype(BF16),
                              w_pool[l].astype(BF16), pool_scale[l].reshape(1, d),
                              w_out[l].astype(BF16), norm2[l].reshape(1, d), w_router[l], b_router[l], seq)
        xt = _moe(h2, comb, x1, mod,
                  w1[l].reshape(N_EXPERTS, d, EXPERT_HIDDEN).astype(BF16),
                  w3[l].reshape(N_EXPERTS, d, EXPERT_HIDDEN).astype(BF16),
                  w2[l].reshape(N_EXPERTS, EXPERT_HIDDEN, d).astype(BF16), seq)
    return xt.reshape(bsz, seq, d)
```

```python
import functools

import jax
import jax.numpy as jnp
from jax import lax
from jax.experimental import pallas as pl
from jax.experimental.pallas import tpu as pltpu

F32 = jnp.float32
BF16 = jnp.bfloat16

D_MODEL = 1024
N_HEADS = 8
HEAD_DIM = 64
ATTN_WIDTH = N_HEADS * HEAD_DIM
MOBA_BLOCK = 256
MOBA_TOPK = 3
POOL_WINDOWS = (2, 4, 8, 16)
N_POOL_GROUPS = len(POOL_WINDOWS)
POOL_WIDTH = D_MODEL // 2
POOL_GROUP_IN = POOL_WIDTH // N_POOL_GROUPS
POOL_GROUP_OUT = D_MODEL // N_POOL_GROUPS
N_EXPERT_GROUPS = 4
EXPERTS_PER_GROUP = 8
N_EXPERTS = N_EXPERT_GROUPS * EXPERTS_PER_GROUP
EXPERT_HIDDEN = 256
N_ADA = 6
EPS = 1e-6

LANES = 128
POOL_HALO = 16
QKV_WIDTH = 3 * ATTN_WIDTH
GATE_OFF = QKV_WIDTH + POOL_WIDTH
VMEM_LIMIT = 56 * 1024 * 1024

TM_IN = 512
TM_MERGE = 512
TN_ADA = 1536
RUN_ALIGN = 16
SORT_USED = 2 * TM_MERGE + N_EXPERTS * (RUN_ALIGN - 1)
SPARE_BLOCKS = 2
SORT_ROWS = SORT_USED + SPARE_BLOCKS * RUN_ALIGN
TR_MOE = 512
BLK_PER_TILE = TR_MOE // RUN_ALIGN


def _params(*sem):
    return pltpu.CompilerParams(dimension_semantics=sem, vmem_limit_bytes=VMEM_LIMIT)


def _ada_kernel(c_ref, w_ref, b_ref, o_ref):
    c = c_ref[...]
    c_act = (c * jax.nn.sigmoid(c)).astype(BF16)
    o_ref[...] = jnp.dot(c_act, w_ref[...].astype(BF16), preferred_element_type=F32) + b_ref[...]


def _ada(c, w_ada, b_ada):
    n_layers, d, n = w_ada.shape
    b = c.shape[0]
    return pl.pallas_call(
        _ada_kernel,
        grid=(n_layers, n // TN_ADA),
        in_specs=[
            pl.BlockSpec((b, d), lambda l, j: (0, 0)),
            pl.BlockSpec((None, d, TN_ADA), lambda l, j: (l, 0, j)),
            pl.BlockSpec((None, 1, TN_ADA), lambda l, j: (l, 0, j)),
        ],
        out_specs=pl.BlockSpec((None, b, TN_ADA), lambda l, j: (l, 0, j)),
        out_shape=jax.ShapeDtypeStruct((n_layers, b, n), F32),
        compiler_params=_params("parallel", "parallel"),
        name="ada",
    )(c, w_ada, b_ada.reshape(n_layers, 1, n))


def _modulated_norm(x, gain, shift, scale):
    ms = jnp.mean(x * x, axis=-1, keepdims=True)
    xn = x * lax.rsqrt(ms + EPS) * gain
    return xn * (1.0 + scale) + shift


def _in_kernel(x_ref, mod_ref, g_ref, w_ref, qkv_ref, pin_ref, gates_ref):
    h = _modulated_norm(x_ref[...], g_ref[...], mod_ref[:, 0:D_MODEL],
                        mod_ref[:, D_MODEL:2 * D_MODEL]).astype(BF16)
    qkv_ref[...] = jnp.dot(h, w_ref[:, 0:QKV_WIDTH], preferred_element_type=F32).astype(BF16)
    pin_ref[...] = jnp.dot(h, w_ref[:, QKV_WIDTH:GATE_OFF], preferred_element_type=F32).astype(BF16)
    gates_ref[...] = jnp.dot(h, w_ref[:, GATE_OFF:], preferred_element_type=F32).astype(BF16)


def _in_proj(x, mod, gain, w_in, seq):
    t, d = x.shape
    n = w_in.shape[1]
    tiles_per_seq = seq // TM_IN
    return pl.pallas_call(
        _in_kernel,
        grid=(t // TM_IN,),
        in_specs=[
            pl.BlockSpec((TM_IN, d), lambda i: (i, 0)),
            pl.BlockSpec((None, 1, N_ADA * d), lambda i: (i // tiles_per_seq, 0, 0)),
            pl.BlockSpec((1, d), lambda i: (0, 0)),
            pl.BlockSpec((d, n), lambda i: (0, 0)),
        ],
        out_specs=[
            pl.BlockSpec((TM_IN, QKV_WIDTH), lambda i: (i, 0)),
            pl.BlockSpec((TM_IN, POOL_WIDTH), lambda i: (i, 0)),
            pl.BlockSpec((TM_IN, 2 * d), lambda i: (i, 0)),
        ],
        out_shape=[
            jax.ShapeDtypeStruct((t, QKV_WIDTH), BF16),
            jax.ShapeDtypeStruct((t, POOL_WIDTH), BF16),
            jax.ShapeDtypeStruct((t, 2 * d), BF16),
        ],
        compiler_params=_params("parallel"),
        name="in_proj",
    )(x, mod, gain, w_in)


def _att_kernel(q_ref, k_ref, v_ref, qg_ref, kg_ref, o_ref, kn_ref, km_ref, *, seq):
    nb = seq // MOBA_BLOCK
    lane = lax.broadcasted_iota(jnp.int32, (1, LANES), 1)
    m_a = (lane < HEAD_DIM).astype(F32)
    m_b = 1.0 - m_a

    def head_norm(z, gain):
        z2 = z * z
        ss_a = jnp.sum(z2 * m_a, axis=-1, keepdims=True)
        ss_b = jnp.sum(z2 * m_b, axis=-1, keepdims=True)
        r = m_a * lax.rsqrt(ss_a * (1.0 / HEAD_DIM) + EPS) + m_b * lax.rsqrt(ss_b * (1.0 / HEAD_DIM) + EPS)
        return z * r * gain

    for j in range(nb):
        rows = pl.ds(j * MOBA_BLOCK, MOBA_BLOCK)
        kn = head_norm(k_ref[rows, :].astype(F32), kg_ref[...])
        kn_ref[rows, :] = kn.astype(BF16)
        km_ref[j:j + 1, :] = jnp.mean(kn, axis=0, keepdims=True)

    blk = lax.broadcasted_iota(jnp.int32, (1, nb), 1)
    row_id = lax.broadcasted_iota(jnp.int32, (MOBA_BLOCK, MOBA_BLOCK), 0)
    col_id = lax.broadcasted_iota(jnp.int32, (MOBA_BLOCK, MOBA_BLOCK), 1)
    causal = col_id <= row_id
    nt = (((1,), (1,)), ((), ()))
    neg_inf = -jnp.inf

    for i in range(nb):
        qrows = pl.ds(i * MOBA_BLOCK, MOBA_BLOCK)
        qn = head_norm(q_ref[qrows, :].astype(F32), qg_ref[...])
        outs = []
        for m_h in (m_a, m_b):
            qh = qn * m_h
            if i > 0:
                gate = lax.dot_general(qh, km_ref[...], nt, precision=lax.Precision.HIGHEST,
                                       preferred_element_type=F32)
                gate = jnp.where(blk < i, gate, neg_inf)
                rank = jnp.zeros(gate.shape, jnp.int32)
                for l in range(i):
                    g_l = gate[:, l:l + 1]
                    ahead = (g_l > gate) | ((g_l == gate) & (l < blk))
                    rank = rank + ahead.astype(jnp.int32)
                chosen = (rank < MOBA_TOPK) & (blk < i)
            qs = (qh * (HEAD_DIM ** -0.5)).astype(BF16)
            s_blocks = []
            m_run = None
            for j in range(i + 1):
                krows = pl.ds(j * MOBA_BLOCK, MOBA_BLOCK)
                s = lax.dot_general(qs, kn_ref[krows, :], nt, preferred_element_type=F32)
                if j < i:
                    s = jnp.where(chosen[:, j:j + 1], s, neg_inf)
                else:
                    s = jnp.where(causal, s, neg_inf)
                s_blocks.append(s)
                m_run = s if m_run is None else jnp.maximum(m_run, s)
            m_row = jnp.max(m_run, axis=-1, keepdims=True)
            l_run = None
            acc = None
            for j in range(i + 1):
                krows = pl.ds(j * MOBA_BLOCK, MOBA_BLOCK)
                p = jnp.exp(s_blocks[j] - m_row)
                l_run = p if l_run is None else l_run + p
                pv = jnp.dot(p.astype(BF16), v_ref[krows, :], preferred_element_type=F32)
                acc = pv if acc is None else acc + pv
            l_row = jnp.sum(l_run, axis=-1, keepdims=True)
            outs.append(acc / l_row)
        o_ref[qrows, :] = (outs[0] * m_a + outs[1] * m_b).astype(o_ref.dtype)


def _attention(qkv, q_gain, k_gain, seq):
    t = qkv.shape[0]
    n_pairs = ATTN_WIDTH // LANES
    kern = functools.partial(_att_kernel, seq=seq)
    return pl.pallas_call(
        kern,
        grid=(t // seq, n_pairs),
        in_specs=[
            pl.BlockSpec((seq, LANES), lambda b, h: (b, h)),
            pl.BlockSpec((seq, LANES), lambda b, h: (b, n_pairs + h)),
            pl.BlockSpec((seq, LANES), lambda b, h: (b, 2 * n_pairs + h)),
            pl.BlockSpec((1, LANES), lambda b, h: (0, 0)),
            pl.BlockSpec((1, LANES), lambda b, h: (0, 0)),
        ],
        out_specs=pl.BlockSpec((seq, LANES), lambda b, h: (b, h)),
        out_shape=jax.ShapeDtypeStruct((t, ATTN_WIDTH), BF16),
        scratch_shapes=[
            pltpu.VMEM((seq, LANES), BF16),
            pltpu.VMEM((seq // MOBA_BLOCK, LANES), F32),
        ],
        compiler_params=_params("parallel", "parallel"),
        name="moba_attention",
    )(qkv, qkv, qkv, q_gain, k_gain)


def _merge_kernel(x_ref, attn_ref, pin_ref, halo_ref, gates_ref, mod_ref, wup_ref, wpool_ref,
                  pscale_ref, wout_ref, g2_ref, wr_ref, br_ref,
                  x1_ref, hs_ref, meta_ref, cnt_ref, pool_ref, *, seq):
    tm = x_ref.shape[0]
    d = D_MODEL
    seq_pos = (pl.program_id(0) * tm) % seq

    halo = halo_ref[...].astype(F32)
    pool_ref[0:POOL_HALO, :] = jnp.where(seq_pos == 0, 0.0, halo)
    pool_ref[POOL_HALO:, :] = pin_ref[...].astype(F32)
    pos = seq_pos + lax.broadcasted_iota(jnp.int32, (tm, 1), 0)

    a_up = jnp.dot(attn_ref[...], wup_ref[...], preferred_element_type=F32)
    merged = []
    for g, w in enumerate(POOL_WINDOWS):
        cols = slice(g * POOL_GROUP_IN, (g + 1) * POOL_GROUP_IN)
        tok = pool_ref[POOL_HALO:, cols]
        win = tok
        for k in range(1, w):
            win = win + pool_ref[POOL_HALO - k:POOL_HALO - k + tm, cols]
        cnt = jnp.minimum(pos + 1, w).astype(F32)
        pooled = (win / cnt - tok).astype(BF16)
        ocols = slice(g * POOL_GROUP_OUT, (g + 1) * POOL_GROUP_OUT)
        b_pool = jnp.dot(pooled, wpool_ref[g], preferred_element_type=F32) * pscale_ref[:, ocols]
        ga = gates_ref[:, ocols].astype(F32)
        gp = gates_ref[:, d + g * POOL_GROUP_OUT:d + (g + 1) * POOL_GROUP_OUT].astype(F32)
        merged.append((jax.nn.sigmoid(ga) * a_up[:, ocols] + jax.nn.sigmoid(gp) * b_pool).astype(BF16))
    merged = jnp.concatenate(merged, axis=-1)
    y = jnp.dot(merged, wout_ref[...], preferred_element_type=F32)
    x1 = x_ref[...] + mod_ref[:, 2 * d:3 * d] * y
    x1_ref[...] = x1

    h2 = _modulated_norm(x1, g2_ref[...], mod_ref[:, 3 * d:4 * d], mod_ref[:, 4 * d:5 * d])

    logits = jnp.dot(h2, wr_ref[...], precision=lax.Precision.HIGHEST,
                     preferred_element_type=F32) + br_ref[...]
    lane = lax.broadcasted_iota(jnp.int32, (1, LANES), 1)
    neg_inf = -jnp.inf
    is_group = (lane >= N_EXPERTS) & (lane < N_EXPERTS + N_EXPERT_GROUPS)
    gl = jnp.where(is_group, logits, neg_inf)
    g_max = jnp.max(gl, axis=-1, keepdims=True)
    g_top = 1.0 / jnp.sum(jnp.exp(gl - g_max), axis=-1, keepdims=True)
    g_idx = jnp.min(jnp.where(gl == g_max, lane, LANES), axis=-1, keepdims=True) - N_EXPERTS
    in_group = (lane < N_EXPERTS) & ((lane // EXPERTS_PER_GROUP) == g_idx)
    el = jnp.where(in_group, logits, neg_inf)
    e1 = jnp.max(el, axis=-1, keepdims=True)
    i1 = jnp.min(jnp.where(el == e1, lane, LANES), axis=-1, keepdims=True)
    el2 = jnp.where(lane == i1, neg_inf, el)
    e2 = jnp.max(el2, axis=-1, keepdims=True)
    i2 = jnp.min(jnp.where(el2 == e2, lane, LANES), axis=-1, keepdims=True)
    r = jnp.exp(e2 - e1)
    w_first = g_top / (1.0 + r)
    w_second = w_first * r

    onehot1 = jnp.where(lane == i1, 1.0, 0.0)
    onehot2 = jnp.where(lane == i2, 1.0, 0.0)
    t_row = lax.broadcasted_iota(jnp.int32, (tm, tm), 0)
    t_col = lax.broadcasted_iota(jnp.int32, (tm, tm), 1)
    earlier = jnp.where(t_col < t_row, 1.0, 0.0).astype(BF16)
    both = jnp.concatenate([onehot1, onehot2], axis=-1).astype(BF16)
    before = jnp.dot(earlier, both, preferred_element_type=F32)
    before1, before2 = before[:, :LANES], before[:, LANES:]
    cnt1 = jnp.sum(onehot1, axis=0, keepdims=True)
    cnt = cnt1 + jnp.sum(onehot2, axis=0, keepdims=True)
    run_len = ((cnt.astype(jnp.int32) + (RUN_ALIGN - 1)) // RUN_ALIGN * RUN_ALIGN).astype(F32)
    e_row = lax.broadcasted_iota(jnp.int32, (LANES, LANES), 0)
    e_col = lax.broadcasted_iota(jnp.int32, (LANES, LANES), 1)
    lower_experts = jnp.where(e_row < e_col, 1.0, 0.0)
    run_start = jnp.dot(jnp.broadcast_to(run_len, (8, LANES)), lower_experts,
                        precision=lax.Precision.HIGHEST, preferred_element_type=F32)[0:1, :]
    pos1 = jnp.sum(onehot1 * (run_start + before1), axis=-1, keepdims=True)
    pos2 = jnp.sum(onehot2 * (run_start + cnt1 + before2), axis=-1, keepdims=True)
    meta = jnp.where(lane == 0, pos1, jnp.where(lane == 1, pos2,
                     jnp.where(lane == 2, w_first, jnp.where(lane == 3, w_second, 0.0))))
    meta_ref[...] = meta
    cnt_ref[...] = jnp.broadcast_to(cnt, (8, LANES))

    eye8 = jnp.where(lax.broadcasted_iota(jnp.int32, (8, LANES), 0)
                     == lax.broadcasted_iota(jnp.int32, (8, LANES), 1), 1.0, 0.0)
    pos_rows = lax.dot_general(eye8, meta, (((1,), (1,)), ((), ())),
                               precision=lax.Precision.HIGHEST, preferred_element_type=F32)
    r_id = lax.broadcasted_iota(jnp.int32, (SORT_ROWS, tm), 0).astype(F32)
    hit = (r_id == pos_rows[0:1, :]) | (r_id == pos_rows[1:2, :])
    perm = jnp.where(hit, 1.0, 0.0).astype(BF16)
    hs_ref[...] = jnp.dot(perm, h2.astype(BF16), preferred_element_type=F32).astype(BF16)


def _merge(x, attn, pin, gates, mod, w_up, w_pool, pool_scale, w_out, gain2, w_router, b_router, seq):
    t, d = x.shape
    tm = TM_MERGE
    n_tiles = t // tm
    tiles_per_seq = seq // tm
    halo_per_tile = tm // POOL_HALO
    kern = functools.partial(_merge_kernel, seq=seq)
    const2 = lambda i: (0, 0)
    return pl.pallas_call(
        kern,
        grid=(t // tm,),
        in_specs=[
            pl.BlockSpec((tm, d), lambda i: (i, 0)),
            pl.BlockSpec((tm, ATTN_WIDTH), lambda i: (i, 0)),
            pl.BlockSpec((tm, POOL_WIDTH), lambda i: (i, 0)),
            pl.BlockSpec((POOL_HALO, POOL_WIDTH), lambda i: (jnp.maximum(i * halo_per_tile - 1, 0), 0)),
            pl.BlockSpec((tm, 2 * d), lambda i: (i, 0)),
            pl.BlockSpec((None, 1, N_ADA * d), lambda i: (i // tiles_per_seq, 0, 0)),
            pl.BlockSpec((ATTN_WIDTH, d), const2),
            pl.BlockSpec((N_POOL_GROUPS, POOL_GROUP_IN, POOL_GROUP_OUT), lambda i: (0, 0, 0)),
            pl.BlockSpec((1, d), const2),
            pl.BlockSpec((d, d), const2),
            pl.BlockSpec((1, d), const2),
            pl.BlockSpec((d, LANES), const2),
            pl.BlockSpec((1, LANES), const2),
        ],
        out_specs=[
            pl.BlockSpec((tm, d), lambda i: (i, 0)),
            pl.BlockSpec((SORT_ROWS, d), lambda i: (i, 0)),
            pl.BlockSpec((tm, LANES), lambda i: (i, 0)),
            pl.BlockSpec((None, 8, LANES), lambda i: (i, 0, 0)),
        ],
        out_shape=[
            jax.ShapeDtypeStruct((t, d), F32),
            jax.ShapeDtypeStruct((n_tiles * SORT_ROWS, d), BF16),
            jax.ShapeDtypeStruct((t, LANES), F32),
            jax.ShapeDtypeStruct((n_tiles, 8, LANES), F32),
        ],
        scratch_shapes=[pltpu.VMEM((tm + POOL_HALO, POOL_WIDTH), F32)],
        compiler_params=_params("parallel"),
        name="merge_router",
    )(x, attn, pin, pin, gates, mod, w_up, w_pool, pool_scale, w_out, gain2, w_router, b_router)


def _dispatch_plan(cnt, n_tiles_max):
    n_merge = cnt.shape[0]
    run_len = (cnt + (RUN_ALIGN - 1)) // RUN_ALIGN * RUN_ALIGN
    run_start = jnp.cumsum(run_len, axis=1) - run_len
    run_blocks = (run_len // RUN_ALIGN).T
    run_first = jnp.cumsum(run_blocks, axis=1) - run_blocks
    blocks_e = jnp.sum(run_blocks, axis=1)
    tiles_e = (blocks_e + BLK_PER_TILE - 1) // BLK_PER_TILE
    tiles_end = jnp.cumsum(tiles_e)
    n_used = tiles_end[-1]
    tile_id = jnp.arange(n_tiles_max, dtype=jnp.int32)
    tile_e = jnp.sum((tiles_end[None, :] <= tile_id[:, None]).astype(jnp.int32), axis=1)
    tile_e = jnp.minimum(tile_e, N_EXPERTS - 1)
    tile_first = (tiles_end - tiles_e)[tile_e]

    slot = jnp.arange(BLK_PER_TILE, dtype=jnp.int32)[None, :]
    blk = (tile_id - tile_first)[:, None] * BLK_PER_TILE + slot
    e_blk = jnp.broadcast_to(tile_e[:, None], blk.shape)
    valid = (tile_id[:, None] < n_used) & (blk < blocks_e[e_blk])
    ends = (run_first + run_blocks)[e_blk]
    run = jnp.minimum(jnp.sum(ends <= blk[..., None], axis=-1), n_merge - 1)
    src = (run * SORT_ROWS + run_start.T[e_blk, run] + (blk - run_first[e_blk, run]) * RUN_ALIGN)
    assert n_merge * SPARE_BLOCKS >= 2 * BLK_PER_TILE
    k = (tile_id[:, None] % 2) * BLK_PER_TILE + slot
    trash = (k // SPARE_BLOCKS) * SORT_ROWS + SORT_USED + (k % SPARE_BLOCKS) * RUN_ALIGN
    dst = jnp.where(valid, src, trash).astype(jnp.int32).reshape(-1)
    src = jnp.where(valid, src, jnp.where(valid[:, 0:1], src[:, 0:1], 0)).astype(jnp.int32).reshape(-1)
    return tile_e, n_used.astype(jnp.int32).reshape(1), src, dst


def _expert_kernel(tile_e_ref, n_used_ref, src_ref, dst_ref,
                   hs_ref, w1_ref, w3_ref, w2_ref, ys_ref,
                   xbuf, ybuf, w1b, w3b, w2b, gsem, ssem):
    i = pl.program_id(0)
    n_used = n_used_ref[0]
    slot = i % 2

    def gather_copy(tile, s, m):
        src = pl.multiple_of(src_ref[tile * BLK_PER_TILE + m], RUN_ALIGN)
        return pltpu.make_async_copy(hs_ref.at[pl.ds(src, RUN_ALIGN), :],
                                     xbuf.at[s, pl.ds(m * RUN_ALIGN, RUN_ALIGN), :], gsem.at[s])

    def scatter_copy(tile, s, m):
        dst = pl.multiple_of(dst_ref[tile * BLK_PER_TILE + m], RUN_ALIGN)
        return pltpu.make_async_copy(ybuf.at[s, pl.ds(m * RUN_ALIGN, RUN_ALIGN), :],
                                     ys_ref.at[pl.ds(dst, RUN_ALIGN), :], ssem.at[s])

    def start_all(copy, tile, s):
        for m in range(BLK_PER_TILE):
            copy(tile, s, m).start()

    def wait_all(copy, tile, s):
        for m in range(BLK_PER_TILE):
            copy(tile, s, m).wait()

    @pl.when(i == 0)
    def _():
        start_all(gather_copy, 0, 0)

    @pl.when(i + 1 < n_used)
    def _():
        start_all(gather_copy, i + 1, 1 - slot)

    @pl.when(i < n_used)
    def _():
        e = tile_e_ref[i]
        e_prev = tile_e_ref[jnp.maximum(i - 1, 0)]

        @pl.when((i == 0) | (e != e_prev))
        def _():
            w1b[...] = w1_ref[...].astype(BF16)
            w3b[...] = w3_ref[...].astype(BF16)
            w2b[...] = w2_ref[...].astype(BF16)

        wait_all(gather_copy, i, slot)

        @pl.when(i >= 2)
        def _():
            wait_all(scatter_copy, i - 2, slot)

        x = xbuf[slot]
        a = jnp.dot(x, w1b[...], preferred_element_type=F32)
        b = jnp.dot(x, w3b[...], preferred_element_type=F32)
        hid = (a * jax.nn.sigmoid(a) * b).astype(BF16)
        ybuf[slot] = jnp.dot(hid, w2b[...], preferred_element_type=F32).astype(BF16)
        start_all(scatter_copy, i, slot)

        @pl.when(i == n_used - 1)
        def _():
            @pl.when(i >= 1)
            def _():
                wait_all(scatter_copy, i - 1, 1 - slot)
            wait_all(scatter_copy, i, slot)


def _experts(hs, cnt, w1, w3, w2):
    n_rows, d = hs.shape
    n_exp, _, f = w1.shape
    n_tiles_max = n_rows // TR_MOE + N_EXPERTS
    tile_e, n_used, src, dst = _dispatch_plan(cnt, n_tiles_max)
    w_in_map = lambda i, te, nu, s, dd: (te[i], 0, 0)
    grid_spec = pltpu.PrefetchScalarGridSpec(
        num_scalar_prefetch=4,
        grid=(n_tiles_max,),
        in_specs=[
            pl.BlockSpec(memory_space=pl.ANY),
            pl.BlockSpec((None, d, f), w_in_map),
            pl.BlockSpec((None, d, f), w_in_map),
            pl.BlockSpec((None, f, d), w_in_map),
        ],
        out_specs=pl.BlockSpec(memory_space=pl.ANY),
        scratch_shapes=[
            pltpu.VMEM((2, TR_MOE, d), BF16),
            pltpu.VMEM((2, TR_MOE, d), BF16),
            pltpu.VMEM((d, f), BF16),
            pltpu.VMEM((d, f), BF16),
            pltpu.VMEM((f, d), BF16),
            pltpu.SemaphoreType.DMA((2,)),
            pltpu.SemaphoreType.DMA((2,)),
        ],
    )
    return pl.pallas_call(
        _expert_kernel,
        grid_spec=grid_spec,
        out_shape=jax.ShapeDtypeStruct(hs.shape, hs.dtype),
        input_output_aliases={4: 0},
        compiler_params=_params("arbitrary"),
        name="experts",
    )(tile_e, n_used, src, dst, hs, w1, w3, w2)


def _combine_kernel(ys_ref, meta_ref, x1_ref, mod_ref, o_ref):
    tm = x1_ref.shape[0]
    d = D_MODEL
    meta = meta_ref[...]
    r_id = lax.broadcasted_iota(jnp.int32, (tm, SORT_ROWS), 1).astype(F32)
    ys = ys_ref[...]
    pick1 = jnp.where(r_id == meta[:, 0:1], 1.0, 0.0).astype(BF16)
    pick2 = jnp.where(r_id == meta[:, 1:2], 1.0, 0.0).astype(BF16)
    y = (meta[:, 2:3] * jnp.dot(pick1, ys, preferred_element_type=F32)
         + meta[:, 3:4] * jnp.dot(pick2, ys, preferred_element_type=F32))
    o_ref[...] = x1_ref[...] + mod_ref[:, 5 * d:6 * d] * y


def _combine(ys, meta, x1, mod, seq):
    t, d = x1.shape
    tm = TM_MERGE
    tiles_per_seq = seq // tm
    return pl.pallas_call(
        _combine_kernel,
        grid=(t // tm,),
        in_specs=[
            pl.BlockSpec((SORT_ROWS, d), lambda i: (i, 0)),
            pl.BlockSpec((tm, LANES), lambda i: (i, 0)),
            pl.BlockSpec((tm, d), lambda i: (i, 0)),
            pl.BlockSpec((None, 1, N_ADA * d), lambda i: (i // tiles_per_seq, 0, 0)),
        ],
        out_specs=pl.BlockSpec((tm, d), lambda i: (i, 0)),
        out_shape=jax.ShapeDtypeStruct((t, d), F32),
        compiler_params=_params("parallel"),
        name="combine",
    )(ys, meta, x1, mod)


@jax.jit
def kernel(x, c, w_ada, b_ada, norm1, w_in, q_norm, k_norm, w_attn_up, w_pool, pool_scale, w_out,
           norm2, w_router_g, b_router_g, w_router_e, b_router_e, w1, w3, w2):
    bsz, seq, d = x.shape
    n_layers = w_ada.shape[0]
    t = bsz * seq
    mod_all = _ada(c, w_ada, b_ada).reshape(n_layers, bsz, 1, N_ADA * d)

    pad = LANES - N_EXPERTS - N_EXPERT_GROUPS
    w_router = jnp.concatenate(
        [w_router_e, w_router_g, jnp.zeros((n_layers, d, pad), F32)], axis=-1)
    b_router = jnp.concatenate(
        [b_router_e, b_router_g, jnp.zeros((n_layers, pad), F32)], axis=-1).reshape(n_layers, 1, LANES)
    heads_per_block = LANES // HEAD_DIM
    q_gain = jnp.tile(q_norm, (1, heads_per_block)).reshape(n_layers, 1, LANES)
    k_gain = jnp.tile(k_norm, (1, heads_per_block)).reshape(n_layers, 1, LANES)

    xt = x.reshape(t, d)
    for l in range(n_layers):
        mod = mod_all[l]
        qkv, pin, gates = _in_proj(xt, mod, norm1[l].reshape(1, d), w_in[l].astype(BF16), seq)
        attn = _attention(qkv, q_gain[l], k_gain[l], seq)
        x1, hs, meta, cnt = _merge(xt, attn, pin, gates, mod, w_attn_up[l].astype(BF16),
                                   w_pool[l].astype(BF16), pool_scale[l].reshape(1, d),
                                   w_out[l].astype(BF16), norm2[l].reshape(1, d), w_router[l], b_router[l], seq)
        cnt = cnt[:, 0, :N_EXPERTS].astype(jnp.int32)
        ys = _experts(hs, cnt,
                      w1[l].reshape(N_EXPERTS, d, EXPERT_HIDDEN),
                      w3[l].reshape(N_EXPERTS, d, EXPERT_HIDDEN),
                      w2[l].reshape(N_EXPERTS, EXPERT_HIDDEN, d))
        xt = _combine(ys, meta, x1, mod, seq)
    return xt.reshape(bsz, seq, d)
```

```python
import functools

import jax
import jax.numpy as jnp
from jax import lax
from jax.experimental import pallas as pl
from jax.experimental.pallas import tpu as pltpu

F32 = jnp.float32
BF16 = jnp.bfloat16

D_MODEL = 1024
N_HEADS = 8
HEAD_DIM = 64
ATTN_WIDTH = N_HEADS * HEAD_DIM
MOBA_BLOCK = 256
MOBA_TOPK = 3
POOL_WINDOWS = (2, 4, 8, 16)
N_POOL_GROUPS = len(POOL_WINDOWS)
POOL_WIDTH = D_MODEL // 2
POOL_GROUP_IN = POOL_WIDTH // N_POOL_GROUPS
POOL_GROUP_OUT = D_MODEL // N_POOL_GROUPS
N_EXPERT_GROUPS = 4
EXPERTS_PER_GROUP = 8
N_EXPERTS = N_EXPERT_GROUPS * EXPERTS_PER_GROUP
EXPERT_HIDDEN = 256
N_ADA = 6
EPS = 1e-6
LOG2_E = 1.4426950408889634
MASK_BIAS = -1e30

LANES = 128
POOL_HALO = 16
QKV_WIDTH = 3 * ATTN_WIDTH
GATE_OFF = QKV_WIDTH + POOL_WIDTH
VMEM_LIMIT = 56 * 1024 * 1024

TM_IN = 512
TM_MERGE = 512
TN_ADA = 1536
RUN_ALIGN = 16
SORT_USED = 2 * TM_MERGE + N_EXPERTS * (RUN_ALIGN - 1)
SPARE_BLOCKS = 2
SORT_ROWS = SORT_USED + SPARE_BLOCKS * RUN_ALIGN
TR_MOE = 512
BLK_PER_TILE = TR_MOE // RUN_ALIGN


def _params(*sem):
    return pltpu.CompilerParams(dimension_semantics=sem, vmem_limit_bytes=VMEM_LIMIT)


def _ada_kernel(c_ref, w_ref, b_ref, o_ref):
    c = c_ref[...]
    c_act = (c * jax.nn.sigmoid(c)).astype(BF16)
    o_ref[...] = jnp.dot(c_act, w_ref[...].astype(BF16), preferred_element_type=F32) + b_ref[...]


def _ada(c, w_ada, b_ada):
    n_layers, d, n = w_ada.shape
    b = c.shape[0]
    return pl.pallas_call(
        _ada_kernel,
        grid=(n_layers, n // TN_ADA),
        in_specs=[
            pl.BlockSpec((b, d), lambda l, j: (0, 0)),
            pl.BlockSpec((None, d, TN_ADA), lambda l, j: (l, 0, j)),
            pl.BlockSpec((None, 1, TN_ADA), lambda l, j: (l, 0, j)),
        ],
        out_specs=pl.BlockSpec((None, b, TN_ADA), lambda l, j: (l, 0, j)),
        out_shape=jax.ShapeDtypeStruct((n_layers, b, n), F32),
        compiler_params=_params("parallel", "parallel"),
        name="ada",
    )(c, w_ada, b_ada.reshape(n_layers, 1, n))


def _modulated_norm(x, gain, shift, scale):
    ms = jnp.mean(x * x, axis=-1, keepdims=True)
    xn = x * lax.rsqrt(ms + EPS) * gain
    return xn * (1.0 + scale) + shift


def _in_kernel(x_ref, mod_ref, g_ref, w_ref, qkv_ref, pin_ref, gates_ref):
    h = _modulated_norm(x_ref[...], g_ref[...], mod_ref[:, 0:D_MODEL],
                        mod_ref[:, D_MODEL:2 * D_MODEL]).astype(BF16)
    qkv_ref[...] = jnp.dot(h, w_ref[:, 0:QKV_WIDTH], preferred_element_type=F32).astype(BF16)
    pin_ref[...] = jnp.dot(h, w_ref[:, QKV_WIDTH:GATE_OFF], preferred_element_type=F32).astype(BF16)
    gates_ref[...] = jnp.dot(h, w_ref[:, GATE_OFF:], preferred_element_type=F32).astype(BF16)


def _in_proj(x, mod, gain, w_in, seq):
    t, d = x.shape
    n = w_in.shape[1]
    tiles_per_seq = seq // TM_IN
    return pl.pallas_call(
        _in_kernel,
        grid=(t // TM_IN,),
        in_specs=[
            pl.BlockSpec((TM_IN, d), lambda i: (i, 0)),
            pl.BlockSpec((None, 1, N_ADA * d), lambda i: (i // tiles_per_seq, 0, 0)),
            pl.BlockSpec((1, d), lambda i: (0, 0)),
            pl.BlockSpec((d, n), lambda i: (0, 0)),
        ],
        out_specs=[
            pl.BlockSpec((TM_IN, QKV_WIDTH), lambda i: (i, 0)),
            pl.BlockSpec((TM_IN, POOL_WIDTH), lambda i: (i, 0)),
            pl.BlockSpec((TM_IN, 2 * d), lambda i: (i, 0)),
        ],
        out_shape=[
            jax.ShapeDtypeStruct((t, QKV_WIDTH), BF16),
            jax.ShapeDtypeStruct((t, POOL_WIDTH), BF16),
            jax.ShapeDtypeStruct((t, 2 * d), BF16),
        ],
        compiler_params=_params("parallel"),
        name="in_proj",
    )(x, mod, gain, w_in)


def _att_kernel(q_ref, k_ref, v_ref, qg_ref, kg_ref, o_ref, ka_ref, kb_ref, qa_ref, qb_ref, km_ref, *, seq):
    nb = seq // MOBA_BLOCK
    lane = lax.broadcasted_iota(jnp.int32, (1, LANES), 1)
    is_a = lane < HEAD_DIM
    m_a = is_a.astype(F32)
    m_b = 1.0 - m_a
    nt = (((1,), (1,)), ((), ()))
    neg_inf = -jnp.inf

    def head_norm(z, gain):
        z2 = z * z
        ss_a = jnp.sum(z2 * m_a, axis=-1, keepdims=True)
        ss_b = jnp.sum(z2 * m_b, axis=-1, keepdims=True)
        r = jnp.where(is_a, lax.rsqrt(ss_a * (1.0 / HEAD_DIM) + EPS), lax.rsqrt(ss_b * (1.0 / HEAD_DIM) + EPS))
        return z * (r * gain)

    for j in range(nb):
        rows = pl.ds(j * MOBA_BLOCK, MOBA_BLOCK)
        kn = head_norm(k_ref[rows, :].astype(F32), kg_ref[...])
        km_ref[j:j + 1, :] = jnp.mean(kn, axis=0, keepdims=True)
        ka_ref[rows, :] = jnp.where(is_a, kn, (lane == HEAD_DIM + j).astype(F32)).astype(BF16)
        kb_ref[rows, :] = jnp.where(is_a, (lane == j).astype(F32), kn).astype(BF16)

    km = km_ref[...]
    km_both = jnp.concatenate([km * m_a, km * m_b], axis=0)
    blk = lax.broadcasted_iota(jnp.int32, (nb, 1), 0)
    row_id = lax.broadcasted_iota(jnp.int32, (MOBA_BLOCK, MOBA_BLOCK), 0)
    col_id = lax.broadcasted_iota(jnp.int32, (MOBA_BLOCK, MOBA_BLOCK), 1)
    causal = col_id <= row_id
    eye = jnp.where(col_id == row_id, 1.0, 0.0).astype(BF16)

    qn_all = head_norm(q_ref[...].astype(F32), qg_ref[...])
    gates = lax.dot_general(km_both, qn_all, nt, precision=lax.Precision.HIGHEST,
                            preferred_element_type=F32)
    own = lax.broadcasted_iota(jnp.int32, (1, seq), 1) // MOBA_BLOCK
    bias_t = []
    for h in range(2):
        gate = jnp.where(blk < own, gates[h * nb:(h + 1) * nb, :], neg_inf)
        rank = jnp.zeros(gate.shape, jnp.int32)
        for l in range(nb - 1):
            g_l = gate[l:l + 1, :]
            ahead = (g_l > gate) | ((g_l == gate) & (l < blk))
            rank = rank + ahead.astype(jnp.int32)
        attended = ((rank < MOBA_TOPK) & (blk < own)) | (blk == own)
        bias_t.append(jnp.where(attended, 0.0, MASK_BIAS))
    pad = jnp.zeros((HEAD_DIM - nb, seq), F32)
    bias_t = jnp.concatenate([bias_t[1], pad, bias_t[0], pad], axis=0).astype(BF16)
    for i in range(nb):
        qrows = pl.ds(i * MOBA_BLOCK, MOBA_BLOCK)
        bias = lax.dot_general(eye, bias_t[:, i * MOBA_BLOCK:(i + 1) * MOBA_BLOCK], nt,
                               preferred_element_type=F32)
        q_aug = qn_all[i * MOBA_BLOCK:(i + 1) * MOBA_BLOCK] * (HEAD_DIM ** -0.5 * LOG2_E)
        qa_ref[qrows, :] = jnp.where(is_a, q_aug, bias).astype(BF16)
        qb_ref[qrows, :] = jnp.where(is_a, bias, q_aug).astype(BF16)

    for i in range(nb):
        qrows = pl.ds(i * MOBA_BLOCK, MOBA_BLOCK)
        outs = []
        for q_ref_h, k_ref_h in ((qa_ref, ka_ref), (qb_ref, kb_ref)):
            qh = q_ref_h[qrows, :]
            s_blocks = []
            m_run = None
            for j in range(i + 1):
                krows = pl.ds(j * MOBA_BLOCK, MOBA_BLOCK)
                s = lax.dot_general(qh, k_ref_h[krows, :], nt, preferred_element_type=F32)
                if j == i:
                    s = jnp.where(causal, s, neg_inf)
                s_blocks.append(s)
                m_blk = jnp.maximum(s[:, :LANES], s[:, LANES:])
                m_run = m_blk if m_run is None else jnp.maximum(m_run, m_blk)
            m_row = jnp.max(m_run, axis=-1, keepdims=True)
            l_run = None
            p_blocks = []
            for j in range(i + 1):
                p = jnp.exp2(s_blocks[j] - m_row)
                l_blk = p[:, :LANES] + p[:, LANES:]
                l_run = l_blk if l_run is None else l_run + l_blk
                p_blocks.append(p.astype(BF16))
            p_all = p_blocks[0] if i == 0 else jnp.concatenate(p_blocks, axis=-1)
            acc = jnp.dot(p_all, v_ref[0:(i + 1) * MOBA_BLOCK, :], preferred_element_type=F32)
            l_row = jnp.sum(l_run, axis=-1, keepdims=True)
            outs.append(acc * (1.0 / l_row))
        o_ref[qrows, :] = jnp.where(is_a, outs[0], outs[1]).astype(o_ref.dtype)


def _attention(qkv, q_gain, k_gain, seq):
    t = qkv.shape[0]
    n_pairs = ATTN_WIDTH // LANES
    kern = functools.partial(_att_kernel, seq=seq)
    return pl.pallas_call(
        kern,
        grid=(t // seq, n_pairs),
        in_specs=[
            pl.BlockSpec((seq, LANES), lambda b, h: (b, h)),
            pl.BlockSpec((seq, LANES), lambda b, h: (b, n_pairs + h)),
            pl.BlockSpec((seq, LANES), lambda b, h: (b, 2 * n_pairs + h)),
            pl.BlockSpec((1, LANES), lambda b, h: (0, 0)),
            pl.BlockSpec((1, LANES), lambda b, h: (0, 0)),
        ],
        out_specs=pl.BlockSpec((seq, LANES), lambda b, h: (b, h)),
        out_shape=jax.ShapeDtypeStruct((t, ATTN_WIDTH), BF16),
        scratch_shapes=[
            pltpu.VMEM((seq, LANES), BF16),
            pltpu.VMEM((seq, LANES), BF16),
            pltpu.VMEM((seq, LANES), BF16),
            pltpu.VMEM((seq, LANES), BF16),
            pltpu.VMEM((seq // MOBA_BLOCK, LANES), F32),
        ],
        compiler_params=_params("parallel", "parallel"),
        name="moba_attention",
    )(qkv, qkv, qkv, q_gain, k_gain)


def _merge_kernel(x_ref, attn_ref, pin_ref, halo_ref, gates_ref, mod_ref, wup_ref, wpool_ref,
                  pscale_ref, wout_ref, g2_ref, wr_ref, br_ref,
                  x1_ref, hs_ref, meta_ref, cnt_ref, pool_ref, *, seq):
    tm = x_ref.shape[0]
    d = D_MODEL
    seq_pos = (pl.program_id(0) * tm) % seq

    halo = halo_ref[...].astype(F32)
    pool_ref[0:POOL_HALO, :] = jnp.where(seq_pos == 0, 0.0, halo)
    pool_ref[POOL_HALO:, :] = pin_ref[...].astype(F32)
    pos = seq_pos + lax.broadcasted_iota(jnp.int32, (tm, 1), 0)

    a_up = jnp.dot(attn_ref[...], wup_ref[...], preferred_element_type=F32)
    merged = []
    for g, w in enumerate(POOL_WINDOWS):
        cols = slice(g * POOL_GROUP_IN, (g + 1) * POOL_GROUP_IN)
        tok = pool_ref[POOL_HALO:, cols]
        win = tok
        for k in range(1, w):
            win = win + pool_ref[POOL_HALO - k:POOL_HALO - k + tm, cols]
        cnt = jnp.minimum(pos + 1, w).astype(F32)
        pooled = (win / cnt - tok).astype(BF16)
        ocols = slice(g * POOL_GROUP_OUT, (g + 1) * POOL_GROUP_OUT)
        b_pool = jnp.dot(pooled, wpool_ref[g], preferred_element_type=F32) * pscale_ref[:, ocols]
        ga = gates_ref[:, ocols].astype(F32)
        gp = gates_ref[:, d + g * POOL_GROUP_OUT:d + (g + 1) * POOL_GROUP_OUT].astype(F32)
        merged.append((jax.nn.sigmoid(ga) * a_up[:, ocols] + jax.nn.sigmoid(gp) * b_pool).astype(BF16))
    merged = jnp.concatenate(merged, axis=-1)
    y = jnp.dot(merged, wout_ref[...], preferred_element_type=F32)
    x1 = x_ref[...] + mod_ref[:, 2 * d:3 * d] * y
    x1_ref[...] = x1

    h2 = _modulated_norm(x1, g2_ref[...], mod_ref[:, 3 * d:4 * d], mod_ref[:, 4 * d:5 * d])

    h2_hi = h2.astype(BF16)
    h2_lo = (h2 - h2_hi.astype(F32)).astype(BF16)
    wr = wr_ref[...]
    wr_hi = wr.astype(BF16)
    wr_lo = (wr - wr_hi.astype(F32)).astype(BF16)
    logits = (jnp.dot(h2_hi, wr_hi, preferred_element_type=F32)
              + jnp.dot(h2_lo, wr_hi, preferred_element_type=F32)
              + jnp.dot(h2_hi, wr_lo, preferred_element_type=F32)) + br_ref[...]
    lane = lax.broadcasted_iota(jnp.int32, (1, LANES), 1)
    neg_inf = -jnp.inf
    is_group = (lane >= N_EXPERTS) & (lane < N_EXPERTS + N_EXPERT_GROUPS)
    gl = jnp.where(is_group, logits, neg_inf)
    g_max = jnp.max(gl, axis=-1, keepdims=True)
    g_top = 1.0 / jnp.sum(jnp.exp(gl - g_max), axis=-1, keepdims=True)
    g_idx = jnp.min(jnp.where(gl == g_max, lane, LANES), axis=-1, keepdims=True) - N_EXPERTS
    in_group = (lane < N_EXPERTS) & ((lane // EXPERTS_PER_GROUP) == g_idx)
    el = jnp.where(in_group, logits, neg_inf)
    e1 = jnp.max(el, axis=-1, keepdims=True)
    i1 = jnp.min(jnp.where(el == e1, lane, LANES), axis=-1, keepdims=True)
    el2 = jnp.where(lane == i1, neg_inf, el)
    e2 = jnp.max(el2, axis=-1, keepdims=True)
    i2 = jnp.min(jnp.where(el2 == e2, lane, LANES), axis=-1, keepdims=True)
    r = jnp.exp(e2 - e1)
    w_first = g_top / (1.0 + r)
    w_second = w_first * r

    onehot1 = jnp.where(lane == i1, 1.0, 0.0)
    onehot2 = jnp.where(lane == i2, 1.0, 0.0)
    t_row = lax.broadcasted_iota(jnp.int32, (tm, tm), 0)
    t_col = lax.broadcasted_iota(jnp.int32, (tm, tm), 1)
    earlier = jnp.where(t_col < t_row, 1.0, 0.0).astype(BF16)
    both = jnp.concatenate([onehot1, onehot2], axis=-1).astype(BF16)
    before = jnp.dot(earlier, both, preferred_element_type=F32)
    before1, before2 = before[:, :LANES], before[:, LANES:]
    cnt1 = jnp.sum(onehot1, axis=0, keepdims=True)
    cnt = cnt1 + jnp.sum(onehot2, axis=0, keepdims=True)
    run_len = ((cnt.astype(jnp.int32) + (RUN_ALIGN - 1)) // RUN_ALIGN * RUN_ALIGN).astype(F32)
    e_row = lax.broadcasted_iota(jnp.int32, (LANES, LANES), 0)
    e_col = lax.broadcasted_iota(jnp.int32, (LANES, LANES), 1)
    lower_experts = jnp.where(e_row < e_col, 1.0, 0.0)
    run_start = jnp.dot(jnp.broadcast_to(run_len, (8, LANES)), lower_experts,
                        precision=lax.Precision.HIGHEST, preferred_element_type=F32)[0:1, :]
    pos1 = jnp.sum(onehot1 * (run_start + before1), axis=-1, keepdims=True)
    pos2 = jnp.sum(onehot2 * (run_start + cnt1 + before2), axis=-1, keepdims=True)
    meta = jnp.where(lane == 0, pos1, jnp.where(lane == 1, pos2,
                     jnp.where(lane == 2, w_first, jnp.where(lane == 3, w_second, 0.0))))
    meta_ref[...] = meta
    cnt_ref[...] = jnp.broadcast_to(cnt, (8, LANES))

    eye8 = jnp.where(lax.broadcasted_iota(jnp.int32, (8, LANES), 0)
                     == lax.broadcasted_iota(jnp.int32, (8, LANES), 1), 1.0, 0.0)
    pos_rows = lax.dot_general(eye8, meta, (((1,), (1,)), ((), ())),
                               precision=lax.Precision.HIGHEST, preferred_element_type=F32)
    r_id = lax.broadcasted_iota(jnp.int32, (SORT_ROWS, tm), 0).astype(F32)
    hit = (r_id == pos_rows[0:1, :]) | (r_id == pos_rows[1:2, :])
    perm = jnp.where(hit, 1.0, 0.0).astype(BF16)
    hs_ref[...] = jnp.dot(perm, h2_hi, preferred_element_type=F32).astype(BF16)


def _merge(x, attn, pin, gates, mod, w_up, w_pool, pool_scale, w_out, gain2, w_router, b_router, seq):
    t, d = x.shape
    tm = TM_MERGE
    n_tiles = t // tm
    tiles_per_seq = seq // tm
    halo_per_tile = tm // POOL_HALO
    kern = functools.partial(_merge_kernel, seq=seq)
    const2 = lambda i: (0, 0)
    return pl.pallas_call(
        kern,
        grid=(t // tm,),
        in_specs=[
            pl.BlockSpec((tm, d), lambda i: (i, 0)),
            pl.BlockSpec((tm, ATTN_WIDTH), lambda i: (i, 0)),
            pl.BlockSpec((tm, POOL_WIDTH), lambda i: (i, 0)),
            pl.BlockSpec((POOL_HALO, POOL_WIDTH), lambda i: (jnp.maximum(i * halo_per_tile - 1, 0), 0)),
            pl.BlockSpec((tm, 2 * d), lambda i: (i, 0)),
            pl.BlockSpec((None, 1, N_ADA * d), lambda i: (i // tiles_per_seq, 0, 0)),
            pl.BlockSpec((ATTN_WIDTH, d), const2),
            pl.BlockSpec((N_POOL_GROUPS, POOL_GROUP_IN, POOL_GROUP_OUT), lambda i: (0, 0, 0)),
            pl.BlockSpec((1, d), const2),
            pl.BlockSpec((d, d), const2),
            pl.BlockSpec((1, d), const2),
            pl.BlockSpec((d, LANES), const2),
            pl.BlockSpec((1, LANES), const2),
        ],
        out_specs=[
            pl.BlockSpec((tm, d), lambda i: (i, 0)),
            pl.BlockSpec((SORT_ROWS, d), lambda i: (i, 0)),
            pl.BlockSpec((tm, LANES), lambda i: (i, 0)),
            pl.BlockSpec((None, 8, LANES), lambda i: (i, 0, 0)),
        ],
        out_shape=[
            jax.ShapeDtypeStruct((t, d), F32),
            jax.ShapeDtypeStruct((n_tiles * SORT_ROWS, d), BF16),
            jax.ShapeDtypeStruct((t, LANES), F32),
            jax.ShapeDtypeStruct((n_tiles, 8, LANES), F32),
        ],
        scratch_shapes=[pltpu.VMEM((tm + POOL_HALO, POOL_WIDTH), F32)],
        compiler_params=_params("parallel"),
        name="merge_router",
    )(x, attn, pin, pin, gates, mod, w_up, w_pool, pool_scale, w_out, gain2, w_router, b_router)


def _dispatch_plan(cnt, n_tiles_max):
    n_merge = cnt.shape[0]
    i32 = jnp.int32

    def before(n):
        return (jnp.arange(n, dtype=i32)[:, None] < jnp.arange(n, dtype=i32)[None, :]).astype(i32)

    run_len = (cnt + (RUN_ALIGN - 1)) // RUN_ALIGN * RUN_ALIGN
    run_start = jnp.sum(run_len[:, :, None] * before(N_EXPERTS)[None], axis=1)
    run_blocks = (run_len // RUN_ALIGN).T
    run_first = jnp.sum(run_blocks[:, :, None] * before(n_merge)[None], axis=1)
    blocks_e = jnp.sum(run_blocks, axis=1)
    tiles_e = (blocks_e + BLK_PER_TILE - 1) // BLK_PER_TILE
    tiles_first = jnp.sum(tiles_e[:, None] * before(N_EXPERTS), axis=0)
    n_used = jnp.sum(tiles_e)
    tile_id = jnp.arange(n_tiles_max, dtype=i32)
    tile_e = jnp.sum(((tiles_first + tiles_e)[None, :] <= tile_id[:, None]).astype(i32), axis=1)
    tile_e = jnp.minimum(tile_e, N_EXPERTS - 1)
    of_tile = (tile_e[:, None] == jnp.arange(N_EXPERTS, dtype=i32)[None, :]).astype(i32)

    def per_tile(table):
        return jnp.sum(of_tile[:, :, None] * table[None], axis=1)

    tile_first = jnp.sum(of_tile * tiles_first[None, :], axis=1)
    tile_blocks = jnp.sum(of_tile * blocks_e[None, :], axis=1)
    first_t, nblk_t, start_t = per_tile(run_first), per_tile(run_blocks), per_tile(run_start.T)

    slot = jnp.arange(BLK_PER_TILE, dtype=i32)[None, :]
    blk = (tile_id - tile_first)[:, None] * BLK_PER_TILE + slot
    valid = (tile_id[:, None] < n_used) & (blk < tile_blocks[:, None])
    off = blk[:, :, None] - first_t[:, None, :]
    in_run = (off >= 0) & (off < nblk_t[:, None, :])
    row = jnp.arange(n_merge, dtype=i32) * SORT_ROWS + start_t[:, None, :] + off * RUN_ALIGN
    src = jnp.sum(jnp.where(in_run, row, 0), axis=-1)
    assert n_merge * SPARE_BLOCKS >= 2 * BLK_PER_TILE
    k = (tile_id[:, None] % 2) * BLK_PER_TILE + slot
    trash = (k // SPARE_BLOCKS) * SORT_ROWS + SORT_USED + (k % SPARE_BLOCKS) * RUN_ALIGN
    dst = jnp.where(valid, src, trash).astype(jnp.int32).reshape(-1)
    src = jnp.where(valid, src, jnp.where(valid[:, 0:1], src[:, 0:1], 0)).astype(jnp.int32).reshape(-1)
    return tile_e, n_used.astype(jnp.int32).reshape(1), src, dst


def _expert_kernel(tile_e_ref, n_used_ref, src_ref, dst_ref,
                   hs_ref, w1_ref, w3_ref, w2_ref, ys_ref,
                   xbuf, ybuf, w1b, w3b, w2b, gsem, ssem):
    i = pl.program_id(0)
    n_used = n_used_ref[0]
    slot = i % 2

    def gather_copy(tile, s, m):
        src = pl.multiple_of(src_ref[tile * BLK_PER_TILE + m], RUN_ALIGN)
        return pltpu.make_async_copy(hs_ref.at[pl.ds(src, RUN_ALIGN), :],
                                     xbuf.at[s, pl.ds(m * RUN_ALIGN, RUN_ALIGN), :], gsem.at[s])

    def scatter_copy(tile, s, m):
        dst = pl.multiple_of(dst_ref[tile * BLK_PER_TILE + m], RUN_ALIGN)
        return pltpu.make_async_copy(ybuf.at[s, pl.ds(m * RUN_ALIGN, RUN_ALIGN), :],
                                     ys_ref.at[pl.ds(dst, RUN_ALIGN), :], ssem.at[s])

    def start_all(copy, tile, s):
        for m in range(BLK_PER_TILE):
            copy(tile, s, m).start()

    def wait_all(copy, tile, s):
        for m in range(BLK_PER_TILE):
            copy(tile, s, m).wait()

    @pl.when(i == 0)
    def _():
        start_all(gather_copy, 0, 0)

    @pl.when(i + 1 < n_used)
    def _():
        start_all(gather_copy, i + 1, 1 - slot)

    @pl.when(i < n_used)
    def _():
        e = tile_e_ref[i]
        e_prev = tile_e_ref[jnp.maximum(i - 1, 0)]

        @pl.when((i == 0) | (e != e_prev))
        def _():
            w1b[...] = w1_ref[...].astype(BF16)
            w3b[...] = w3_ref[...].astype(BF16)
            w2b[...] = w2_ref[...].astype(BF16)

        wait_all(gather_copy, i, slot)

        @pl.when(i >= 2)
        def _():
            wait_all(scatter_copy, i - 2, slot)

        x = xbuf[slot]
        a = jnp.dot(x, w1b[...], preferred_element_type=F32)
        b = jnp.dot(x, w3b[...], preferred_element_type=F32)
        hid = (a * jax.nn.sigmoid(a) * b).astype(BF16)
        ybuf[slot] = jnp.dot(hid, w2b[...], preferred_element_type=F32).astype(BF16)
        start_all(scatter_copy, i, slot)

        @pl.when(i == n_used - 1)
        def _():
            @pl.when(i >= 1)
            def _():
                wait_all(scatter_copy, i - 1, 1 - slot)
            wait_all(scatter_copy, i, slot)


def _experts(hs, cnt, w1, w3, w2, layer):
    n_rows, d = hs.shape
    f = w1.shape[-1]
    first_expert = layer * N_EXPERTS
    n_tiles_max = n_rows // TR_MOE + N_EXPERTS
    tile_e, n_used, src, dst = _dispatch_plan(cnt, n_tiles_max)
    w_in_map = lambda i, te, nu, s, dd: (first_expert + te[i], 0, 0)
    grid_spec = pltpu.PrefetchScalarGridSpec(
        num_scalar_prefetch=4,
        grid=(n_tiles_max,),
        in_specs=[
            pl.BlockSpec(memory_space=pl.ANY),
            pl.BlockSpec((None, d, f), w_in_map),
            pl.BlockSpec((None, d, f), w_in_map),
            pl.BlockSpec((None, f, d), w_in_map),
        ],
        out_specs=pl.BlockSpec(memory_space=pl.ANY),
        scratch_shapes=[
            pltpu.VMEM((2, TR_MOE, d), BF16),
            pltpu.VMEM((2, TR_MOE, d), BF16),
            pltpu.VMEM((d, f), BF16),
            pltpu.VMEM((d, f), BF16),
            pltpu.VMEM((f, d), BF16),
            pltpu.SemaphoreType.DMA((2,)),
            pltpu.SemaphoreType.DMA((2,)),
        ],
    )
    return pl.pallas_call(
        _expert_kernel,
        grid_spec=grid_spec,
        out_shape=jax.ShapeDtypeStruct(hs.shape, hs.dtype),
        input_output_aliases={4: 0},
        compiler_params=_params("arbitrary"),
        name="experts",
    )(tile_e, n_used, src, dst, hs, w1, w3, w2)


def _combine_kernel(ys_ref, meta_ref, x1_ref, mod_ref, o_ref):
    tm = x1_ref.shape[0]
    d = D_MODEL
    meta = meta_ref[...]
    r_id = lax.broadcasted_iota(jnp.int32, (tm, SORT_ROWS), 1).astype(F32)
    ys = ys_ref[...]
    pick1 = jnp.where(r_id == meta[:, 0:1], 1.0, 0.0).astype(BF16)
    pick2 = jnp.where(r_id == meta[:, 1:2], 1.0, 0.0).astype(BF16)
    y = (meta[:, 2:3] * jnp.dot(pick1, ys, preferred_element_type=F32)
         + meta[:, 3:4] * jnp.dot(pick2, ys, preferred_element_type=F32))
    o_ref[...] = x1_ref[...] + mod_ref[:, 5 * d:6 * d] * y


def _combine(ys, meta, x1, mod, seq):
    t, d = x1.shape
    tm = TM_MERGE
    tiles_per_seq = seq // tm
    return pl.pallas_call(
        _combine_kernel,
        grid=(t // tm,),
        in_specs=[
            pl.BlockSpec((SORT_ROWS, d), lambda i: (i, 0)),
            pl.BlockSpec((tm, LANES), lambda i: (i, 0)),
            pl.BlockSpec((tm, d), lambda i: (i, 0)),
            pl.BlockSpec((None, 1, N_ADA * d), lambda i: (i // tiles_per_seq, 0, 0)),
        ],
        out_specs=pl.BlockSpec((tm, d), lambda i: (i, 0)),
        out_shape=jax.ShapeDtypeStruct((t, d), F32),
        compiler_params=_params("parallel"),
        name="combine",
    )(ys, meta, x1, mod)


@jax.jit
def kernel(x, c, w_ada, b_ada, norm1, w_in, q_norm, k_norm, w_attn_up, w_pool, pool_scale, w_out,
           norm2, w_router_g, b_router_g, w_router_e, b_router_e, w1, w3, w2):
    bsz, seq, d = x.shape
    n_layers = w_ada.shape[0]
    t = bsz * seq
    mod_all = _ada(c, w_ada, b_ada).reshape(n_layers, bsz, 1, N_ADA * d)

    pad = LANES - N_EXPERTS - N_EXPERT_GROUPS
    w_router = jnp.concatenate(
        [w_router_e, w_router_g, jnp.zeros((n_layers, d, pad), F32)], axis=-1)
    b_router = jnp.concatenate(
        [b_router_e, b_router_g, jnp.zeros((n_layers, pad), F32)], axis=-1).reshape(n_layers, 1, LANES)
    heads_per_block = LANES // HEAD_DIM
    q_gain = jnp.tile(q_norm, (1, heads_per_block)).reshape(n_layers, 1, LANES)
    k_gain = jnp.tile(k_norm, (1, heads_per_block)).reshape(n_layers, 1, LANES)

    w1_all = w1.reshape(n_layers * N_EXPERTS, d, EXPERT_HIDDEN)
    w3_all = w3.reshape(n_layers * N_EXPERTS, d, EXPERT_HIDDEN)
    w2_all = w2.reshape(n_layers * N_EXPERTS, EXPERT_HIDDEN, d)

    xt = x.reshape(t, d)
    for l in range(n_layers):
        mod = mod_all[l]
        qkv, pin, gates = _in_proj(xt, mod, norm1[l].reshape(1, d), w_in[l].astype(BF16), seq)
        attn = _attention(qkv, q_gain[l], k_gain[l], seq)
        x1, hs, meta, cnt = _merge(xt, attn, pin, gates, mod, w_attn_up[l].astype(BF16),
                                   w_pool[l].astype(BF16), pool_scale[l].reshape(1, d),
                                   w_out[l].astype(BF16), norm2[l].reshape(1, d), w_router[l], b_router[l], seq)
        cnt = cnt[:, 0, :N_EXPERTS].astype(jnp.int32)
        ys = _experts(hs, cnt, w1_all, w3_all, w2_all, l)
        xt = _combine(ys, meta, x1, mod, seq)
    return xt.reshape(bsz, seq, d)
```

```python
import functools

import jax
import jax.numpy as jnp
from jax import lax
from jax.experimental import pallas as pl
from jax.experimental.pallas import tpu as pltpu

F32 = jnp.float32
BF16 = jnp.bfloat16

D_MODEL = 1024
N_HEADS = 8
HEAD_DIM = 64
ATTN_WIDTH = N_HEADS * HEAD_DIM
MOBA_BLOCK = 256
MOBA_TOPK = 3
POOL_WINDOWS = (2, 4, 8, 16)
N_POOL_GROUPS = len(POOL_WINDOWS)
POOL_WIDTH = D_MODEL // 2
POOL_GROUP_IN = POOL_WIDTH // N_POOL_GROUPS
POOL_GROUP_OUT = D_MODEL // N_POOL_GROUPS
N_EXPERT_GROUPS = 4
EXPERTS_PER_GROUP = 8
N_EXPERTS = N_EXPERT_GROUPS * EXPERTS_PER_GROUP
EXPERT_HIDDEN = 256
N_ADA = 6
EPS = 1e-6
LOG2_E = 1.4426950408889634
MASK_BIAS = -1e30

LANES = 128
POOL_HALO = 16
QKV_WIDTH = 3 * ATTN_WIDTH
GATE_OFF = QKV_WIDTH + POOL_WIDTH
VMEM_LIMIT = 56 * 1024 * 1024

TM_IN = 512
TM_MERGE = 512
TN_ADA = 1536
RUN_ALIGN = 16
SORT_USED = 2 * TM_MERGE + N_EXPERTS * (RUN_ALIGN - 1)
SPARE_BLOCKS = 2
SORT_ROWS = SORT_USED + SPARE_BLOCKS * RUN_ALIGN
TR_MOE = 512
BLK_PER_TILE = TR_MOE // RUN_ALIGN


def _params(*sem):
    return pltpu.CompilerParams(dimension_semantics=sem, vmem_limit_bytes=VMEM_LIMIT)


def _ada_kernel(c_ref, w_ref, b_ref, o_ref):
    c = c_ref[...]
    c_act = (c * jax.nn.sigmoid(c)).astype(BF16)
    o_ref[...] = jnp.dot(c_act, w_ref[...].astype(BF16), preferred_element_type=F32) + b_ref[...]


def _ada(c, w_ada, b_ada):
    n_layers, d, n = w_ada.shape
    b = c.shape[0]
    return pl.pallas_call(
        _ada_kernel,
        grid=(n_layers, n // TN_ADA),
        in_specs=[
            pl.BlockSpec((b, d), lambda l, j: (0, 0)),
            pl.BlockSpec((None, d, TN_ADA), lambda l, j: (l, 0, j)),
            pl.BlockSpec((None, 1, TN_ADA), lambda l, j: (l, 0, j)),
        ],
        out_specs=pl.BlockSpec((None, b, TN_ADA), lambda l, j: (l, 0, j)),
        out_shape=jax.ShapeDtypeStruct((n_layers, b, n), F32),
        compiler_params=_params("parallel", "parallel"),
        name="ada",
    )(c, w_ada, b_ada.reshape(n_layers, 1, n))


def _modulated_norm(x, gain, shift, scale):
    ms = jnp.mean(x * x, axis=-1, keepdims=True)
    xn = x * lax.rsqrt(ms + EPS) * gain
    return xn * (1.0 + scale) + shift


def _in_kernel(x_ref, mod_ref, g_ref, w_ref, qkv_ref, pin_ref, gates_ref):
    h = _modulated_norm(x_ref[...], g_ref[...], mod_ref[:, 0:D_MODEL],
                        mod_ref[:, D_MODEL:2 * D_MODEL]).astype(BF16)
    qkv_ref[...] = jnp.dot(h, w_ref[:, 0:QKV_WIDTH], preferred_element_type=F32).astype(BF16)
    pin_ref[...] = jnp.dot(h, w_ref[:, QKV_WIDTH:GATE_OFF], preferred_element_type=F32).astype(BF16)
    gates_ref[...] = jnp.dot(h, w_ref[:, GATE_OFF:], preferred_element_type=F32).astype(BF16)


def _in_proj(x, mod, gain, w_in, seq):
    t, d = x.shape
    n = w_in.shape[1]
    tiles_per_seq = seq // TM_IN
    return pl.pallas_call(
        _in_kernel,
        grid=(t // TM_IN,),
        in_specs=[
            pl.BlockSpec((TM_IN, d), lambda i: (i, 0)),
            pl.BlockSpec((None, 1, N_ADA * d), lambda i: (i // tiles_per_seq, 0, 0)),
            pl.BlockSpec((1, d), lambda i: (0, 0)),
            pl.BlockSpec((d, n), lambda i: (0, 0)),
        ],
        out_specs=[
            pl.BlockSpec((TM_IN, QKV_WIDTH), lambda i: (i, 0)),
            pl.BlockSpec((TM_IN, POOL_WIDTH), lambda i: (i, 0)),
            pl.BlockSpec((TM_IN, 2 * d), lambda i: (i, 0)),
        ],
        out_shape=[
            jax.ShapeDtypeStruct((t, QKV_WIDTH), BF16),
            jax.ShapeDtypeStruct((t, POOL_WIDTH), BF16),
            jax.ShapeDtypeStruct((t, 2 * d), BF16),
        ],
        compiler_params=_params("parallel"),
        name="in_proj",
    )(x, mod, gain, w_in)


def _att_kernel(q_ref, k_ref, v_ref, qg_ref, kg_ref, o_ref,
                ka_ref, kb_ref, qa_ref, qb_ref, va_ref, vb_ref, km_ref, *, seq):
    nb = seq // MOBA_BLOCK
    lane = lax.broadcasted_iota(jnp.int32, (1, LANES), 1)
    is_a = lane < HEAD_DIM
    m_a = is_a.astype(F32)
    m_b = 1.0 - m_a
    nt = (((1,), (1,)), ((), ()))
    neg_inf = -jnp.inf

    def head_norm(z, gain):
        z2 = z * z
        ss_a = jnp.sum(z2 * m_a, axis=-1, keepdims=True)
        ss_b = jnp.sum(z2 * m_b, axis=-1, keepdims=True)
        r = jnp.where(is_a, lax.rsqrt(ss_a * (1.0 / HEAD_DIM) + EPS), lax.rsqrt(ss_b * (1.0 / HEAD_DIM) + EPS))
        return z * (r * gain)

    for j in range(nb):
        rows = pl.ds(j * MOBA_BLOCK, MOBA_BLOCK)
        kn = head_norm(k_ref[rows, :].astype(F32), kg_ref[...])
        km_ref[j:j + 1, :] = jnp.mean(kn, axis=0, keepdims=True)
        ka_ref[rows, :] = jnp.where(is_a, kn, (lane == HEAD_DIM + j).astype(F32)).astype(BF16)
        kb_ref[rows, :] = jnp.where(is_a, (lane == j).astype(F32), kn).astype(BF16)
        v = v_ref[rows, :]
        va_ref[rows, :] = jnp.where(is_a, v, (lane == HEAD_DIM).astype(BF16))
        vb_ref[rows, :] = jnp.where(is_a, (lane == 0).astype(BF16), v)

    km = km_ref[...]
    km_both = jnp.concatenate([km * m_a, km * m_b], axis=0)
    blk = lax.broadcasted_iota(jnp.int32, (nb, 1), 0)
    row_id = lax.broadcasted_iota(jnp.int32, (MOBA_BLOCK, MOBA_BLOCK), 0)
    col_id = lax.broadcasted_iota(jnp.int32, (MOBA_BLOCK, MOBA_BLOCK), 1)
    causal = col_id <= row_id
    eye = jnp.where(col_id == row_id, 1.0, 0.0).astype(BF16)

    qn_all = head_norm(q_ref[...].astype(F32), qg_ref[...])
    gates = lax.dot_general(km_both, qn_all, nt, precision=lax.Precision.HIGHEST,
                            preferred_element_type=F32)
    own = lax.broadcasted_iota(jnp.int32, (1, seq), 1) // MOBA_BLOCK
    bias_t = []
    for h in range(2):
        gate = jnp.where(blk < own, gates[h * nb:(h + 1) * nb, :], neg_inf)
        rank = jnp.zeros(gate.shape, jnp.int32)
        for l in range(nb - 1):
            g_l = gate[l:l + 1, :]
            ahead = (g_l > gate) | ((g_l == gate) & (l < blk))
            rank = rank + ahead.astype(jnp.int32)
        attended = ((rank < MOBA_TOPK) & (blk < own)) | (blk == own)
        bias_t.append(jnp.where(attended, 0.0, MASK_BIAS))
    pad = jnp.zeros((HEAD_DIM - nb, seq), F32)
    bias_t = jnp.concatenate([bias_t[1], pad, bias_t[0], pad], axis=0).astype(BF16)
    for i in range(nb):
        qrows = pl.ds(i * MOBA_BLOCK, MOBA_BLOCK)
        bias = lax.dot_general(eye, bias_t[:, i * MOBA_BLOCK:(i + 1) * MOBA_BLOCK], nt,
                               preferred_element_type=F32)
        q_aug = qn_all[i * MOBA_BLOCK:(i + 1) * MOBA_BLOCK] * (HEAD_DIM ** -0.5 * LOG2_E)
        qa_ref[qrows, :] = jnp.where(is_a, q_aug, bias).astype(BF16)
        qb_ref[qrows, :] = jnp.where(is_a, bias, q_aug).astype(BF16)

    for i in reversed(range(nb)):
        qrows = pl.ds(i * MOBA_BLOCK, MOBA_BLOCK)
        outs = []
        for q_ref_h, k_ref_h, v_ref_h, sum_lane in ((qa_ref, ka_ref, va_ref, HEAD_DIM),
                                                   (qb_ref, kb_ref, vb_ref, 0)):
            qh = q_ref_h[qrows, :]
            s_blocks = []
            m_run = None
            for j in range(i + 1):
                krows = pl.ds(j * MOBA_BLOCK, MOBA_BLOCK)
                s = lax.dot_general(qh, k_ref_h[krows, :], nt, preferred_element_type=F32)
                if j == i:
                    s = jnp.where(causal, s, neg_inf)
                s_blocks.append(s)
                m_blk = jnp.maximum(s[:, :LANES], s[:, LANES:])
                m_run = m_blk if m_run is None else jnp.maximum(m_run, m_blk)
            m_row = jnp.max(m_run, axis=-1, keepdims=True)
            p_blocks = [jnp.exp2(s - m_row).astype(BF16) for s in s_blocks]
            p_all = p_blocks[0] if i == 0 else jnp.concatenate(p_blocks, axis=-1)
            acc = jnp.dot(p_all, v_ref_h[0:(i + 1) * MOBA_BLOCK, :], preferred_element_type=F32)
            outs.append(acc * (1.0 / acc[:, sum_lane:sum_lane + 1]))
        o_ref[qrows, :] = jnp.where(is_a, outs[0], outs[1]).astype(o_ref.dtype)


def _attention(qkv, q_gain, k_gain, seq):
    t = qkv.shape[0]
    n_pairs = ATTN_WIDTH // LANES
    kern = functools.partial(_att_kernel, seq=seq)
    return pl.pallas_call(
        kern,
        grid=(t // seq, n_pairs),
        in_specs=[
            pl.BlockSpec((seq, LANES), lambda b, h: (b, h)),
            pl.BlockSpec((seq, LANES), lambda b, h: (b, n_pairs + h)),
            pl.BlockSpec((seq, LANES), lambda b, h: (b, 2 * n_pairs + h)),
            pl.BlockSpec((1, LANES), lambda b, h: (0, 0)),
            pl.BlockSpec((1, LANES), lambda b, h: (0, 0)),
        ],
        out_specs=pl.BlockSpec((seq, LANES), lambda b, h: (b, h)),
        out_shape=jax.ShapeDtypeStruct((t, ATTN_WIDTH), BF16),
        scratch_shapes=[
            *[pltpu.VMEM((seq, LANES), BF16)] * 6,
            pltpu.VMEM((seq // MOBA_BLOCK, LANES), F32),
        ],
        compiler_params=_params("parallel", "parallel"),
        name="moba_attention",
    )(qkv, qkv, qkv, q_gain, k_gain)


def _merge_kernel(x_ref, attn_ref, pin_ref, halo_ref, gates_ref, mod_ref, wup_ref, wpool_ref,
                  pscale_ref, wout_ref, g2_ref, wr_ref, br_ref,
                  x1_ref, hs_ref, meta_ref, cnt_ref, pool_ref, *, seq):
    tm = x_ref.shape[0]
    d = D_MODEL
    seq_pos = (pl.program_id(0) * tm) % seq

    halo = halo_ref[...].astype(F32)
    pool_ref[0:POOL_HALO, :] = jnp.where(seq_pos == 0, 0.0, halo)
    pool_ref[POOL_HALO:, :] = pin_ref[...].astype(F32)
    pos = seq_pos + lax.broadcasted_iota(jnp.int32, (tm, 1), 0)

    a_up = jnp.dot(attn_ref[...], wup_ref[...], preferred_element_type=F32)
    merged = []
    for g, w in enumerate(POOL_WINDOWS):
        cols = slice(g * POOL_GROUP_IN, (g + 1) * POOL_GROUP_IN)
        tok = pool_ref[POOL_HALO:, cols]
        win = tok
        for k in range(1, w):
            win = win + pool_ref[POOL_HALO - k:POOL_HALO - k + tm, cols]
        cnt = jnp.minimum(pos + 1, w).astype(F32)
        pooled = (win / cnt - tok).astype(BF16)
        ocols = slice(g * POOL_GROUP_OUT, (g + 1) * POOL_GROUP_OUT)
        b_pool = jnp.dot(pooled, wpool_ref[g], preferred_element_type=F32) * pscale_ref[:, ocols]
        ga = gates_ref[:, ocols].astype(F32)
        gp = gates_ref[:, d + g * POOL_GROUP_OUT:d + (g + 1) * POOL_GROUP_OUT].astype(F32)
        merged.append((jax.nn.sigmoid(ga) * a_up[:, ocols] + jax.nn.sigmoid(gp) * b_pool).astype(BF16))
    merged = jnp.concatenate(merged, axis=-1)
    y = jnp.dot(merged, wout_ref[...], preferred_element_type=F32)
    x1 = x_ref[...] + mod_ref[:, 2 * d:3 * d] * y
    x1_ref[...] = x1

    h2 = _modulated_norm(x1, g2_ref[...], mod_ref[:, 3 * d:4 * d], mod_ref[:, 4 * d:5 * d])

    h2_hi = h2.astype(BF16)
    h2_lo = (h2 - h2_hi.astype(F32)).astype(BF16)
    wr = wr_ref[...]
    wr_hi = wr.astype(BF16)
    wr_lo = (wr - wr_hi.astype(F32)).astype(BF16)
    logits = (jnp.dot(h2_hi, wr_hi, preferred_element_type=F32)
              + jnp.dot(h2_lo, wr_hi, preferred_element_type=F32)
              + jnp.dot(h2_hi, wr_lo, preferred_element_type=F32)) + br_ref[...]
    lane = lax.broadcasted_iota(jnp.int32, (1, LANES), 1)
    neg_inf = -jnp.inf
    is_group = (lane >= N_EXPERTS) & (lane < N_EXPERTS + N_EXPERT_GROUPS)
    gl = jnp.where(is_group, logits, neg_inf)
    g_max = jnp.max(gl, axis=-1, keepdims=True)
    g_top = 1.0 / jnp.sum(jnp.exp(gl - g_max), axis=-1, keepdims=True)
    g_idx = jnp.min(jnp.where(gl == g_max, lane, LANES), axis=-1, keepdims=True) - N_EXPERTS
    in_group = (lane < N_EXPERTS) & ((lane // EXPERTS_PER_GROUP) == g_idx)
    el = jnp.where(in_group, logits, neg_inf)
    e1 = jnp.max(el, axis=-1, keepdims=True)
    i1 = jnp.min(jnp.where(el == e1, lane, LANES), axis=-1, keepdims=True)
    el2 = jnp.where(lane == i1, neg_inf, el)
    e2 = jnp.max(el2, axis=-1, keepdims=True)
    i2 = jnp.min(jnp.where(el2 == e2, lane, LANES), axis=-1, keepdims=True)
    r = jnp.exp(e2 - e1)
    w_first = g_top / (1.0 + r)
    w_second = w_first * r

    onehot1 = jnp.where(lane == i1, 1.0, 0.0)
    onehot2 = jnp.where(lane == i2, 1.0, 0.0)
    t_row = lax.broadcasted_iota(jnp.int32, (tm, tm), 0)
    t_col = lax.broadcasted_iota(jnp.int32, (tm, tm), 1)
    earlier = jnp.where(t_col < t_row, 1.0, 0.0).astype(BF16)
    both = jnp.concatenate([onehot1, onehot2], axis=-1).astype(BF16)
    before = jnp.dot(earlier, both, preferred_element_type=F32)
    before1, before2 = before[:, :LANES], before[:, LANES:]
    cnt1 = jnp.sum(onehot1, axis=0, keepdims=True)
    cnt = cnt1 + jnp.sum(onehot2, axis=0, keepdims=True)
    run_len = ((cnt.astype(jnp.int32) + (RUN_ALIGN - 1)) // RUN_ALIGN * RUN_ALIGN).astype(F32)
    e_row = lax.broadcasted_iota(jnp.int32, (LANES, LANES), 0)
    e_col = lax.broadcasted_iota(jnp.int32, (LANES, LANES), 1)
    lower_experts = jnp.where(e_row < e_col, 1.0, 0.0)
    run_start = jnp.dot(jnp.broadcast_to(run_len, (8, LANES)), lower_experts,
                        precision=lax.Precision.HIGHEST, preferred_element_type=F32)[0:1, :]
    pos1 = jnp.sum(onehot1 * (run_start + before1), axis=-1, keepdims=True)
    pos2 = jnp.sum(onehot2 * (run_start + cnt1 + before2), axis=-1, keepdims=True)
    meta = jnp.where(lane == 0, pos1, jnp.where(lane == 1, pos2,
                     jnp.where(lane == 2, w_first, jnp.where(lane == 3, w_second, 0.0))))
    meta_ref[...] = meta
    cnt_ref[...] = jnp.broadcast_to(cnt, (8, LANES))

    eye8 = jnp.where(lax.broadcasted_iota(jnp.int32, (8, LANES), 0)
                     == lax.broadcasted_iota(jnp.int32, (8, LANES), 1), 1.0, 0.0)
    pos_rows = lax.dot_general(eye8, meta, (((1,), (1,)), ((), ())),
                               precision=lax.Precision.HIGHEST, preferred_element_type=F32)
    r_id = lax.broadcasted_iota(jnp.int32, (SORT_ROWS, tm), 0).astype(F32)
    hit = (r_id == pos_rows[0:1, :]) | (r_id == pos_rows[1:2, :])
    perm = jnp.where(hit, 1.0, 0.0).astype(BF16)
    hs_ref[...] = jnp.dot(perm, h2_hi, preferred_element_type=F32).astype(BF16)


def _merge(x, attn, pin, gates, mod, w_up, w_pool, pool_scale, w_out, gain2, w_router, b_router, seq):
    t, d = x.shape
    tm = TM_MERGE
    n_tiles = t // tm
    tiles_per_seq = seq // tm
    halo_per_tile = tm // POOL_HALO
    kern = functools.partial(_merge_kernel, seq=seq)
    const2 = lambda i: (0, 0)
    return pl.pallas_call(
        kern,
        grid=(t // tm,),
        in_specs=[
            pl.BlockSpec((tm, d), lambda i: (i, 0)),
            pl.BlockSpec((tm, ATTN_WIDTH), lambda i: (i, 0)),
            pl.BlockSpec((tm, POOL_WIDTH), lambda i: (i, 0)),
            pl.BlockSpec((POOL_HALO, POOL_WIDTH), lambda i: (jnp.maximum(i * halo_per_tile - 1, 0), 0)),
            pl.BlockSpec((tm, 2 * d), lambda i: (i, 0)),
            pl.BlockSpec((None, 1, N_ADA * d), lambda i: (i // tiles_per_seq, 0, 0)),
            pl.BlockSpec((ATTN_WIDTH, d), const2),
            pl.BlockSpec((N_POOL_GROUPS, POOL_GROUP_IN, POOL_GROUP_OUT), lambda i: (0, 0, 0)),
            pl.BlockSpec((1, d), const2),
            pl.BlockSpec((d, d), const2),
            pl.BlockSpec((1, d), const2),
            pl.BlockSpec((d, LANES), const2),
            pl.BlockSpec((1, LANES), const2),
        ],
        out_specs=[
            pl.BlockSpec((tm, d), lambda i: (i, 0)),
            pl.BlockSpec((SORT_ROWS, d), lambda i: (i, 0)),
            pl.BlockSpec((tm, LANES), lambda i: (i, 0)),
            pl.BlockSpec((None, 8, LANES), lambda i: (i, 0, 0)),
        ],
        out_shape=[
            jax.ShapeDtypeStruct((t, d), F32),
            jax.ShapeDtypeStruct((n_tiles * SORT_ROWS, d), BF16),
            jax.ShapeDtypeStruct((t, LANES), F32),
            jax.ShapeDtypeStruct((n_tiles, 8, LANES), F32),
        ],
        scratch_shapes=[pltpu.VMEM((tm + POOL_HALO, POOL_WIDTH), F32)],
        compiler_params=_params("parallel"),
        name="merge_router",
    )(x, attn, pin, pin, gates, mod, w_up, w_pool, pool_scale, w_out, gain2, w_router, b_router)


def _dispatch_plan(cnt, n_tiles_max):
    n_merge = cnt.shape[0]
    i32 = jnp.int32

    def before(n):
        return (jnp.arange(n, dtype=i32)[:, None] < jnp.arange(n, dtype=i32)[None, :]).astype(i32)

    run_len = (cnt + (RUN_ALIGN - 1)) // RUN_ALIGN * RUN_ALIGN
    run_start = jnp.sum(run_len[:, :, None] * before(N_EXPERTS)[None], axis=1)
    run_blocks = (run_len // RUN_ALIGN).T
    run_first = jnp.sum(run_blocks[:, :, None] * before(n_merge)[None], axis=1)
    blocks_e = jnp.sum(run_blocks, axis=1)
    tiles_e = (blocks_e + BLK_PER_TILE - 1) // BLK_PER_TILE
    tiles_first = jnp.sum(tiles_e[:, None] * before(N_EXPERTS), axis=0)
    n_used = jnp.sum(tiles_e)
    tile_id = jnp.arange(n_tiles_max, dtype=i32)
    tile_e = jnp.sum(((tiles_first + tiles_e)[None, :] <= tile_id[:, None]).astype(i32), axis=1)
    tile_e = jnp.minimum(tile_e, N_EXPERTS - 1)
    of_tile = (tile_e[:, None] == jnp.arange(N_EXPERTS, dtype=i32)[None, :]).astype(i32)

    def per_tile(table):
        return jnp.sum(of_tile[:, :, None] * table[None], axis=1)

    tile_first = jnp.sum(of_tile * tiles_first[None, :], axis=1)
    tile_blocks = jnp.sum(of_tile * blocks_e[None, :], axis=1)
    first_t, nblk_t, start_t = per_tile(run_first), per_tile(run_blocks), per_tile(run_start.T)

    slot = jnp.arange(BLK_PER_TILE, dtype=i32)[None, :]
    blk = (tile_id - tile_first)[:, None] * BLK_PER_TILE + slot
    valid = (tile_id[:, None] < n_used) & (blk < tile_blocks[:, None])
    off = blk[:, :, None] - first_t[:, None, :]
    in_run = (off >= 0) & (off < nblk_t[:, None, :])
    row = jnp.arange(n_merge, dtype=i32) * SORT_ROWS + start_t[:, None, :] + off * RUN_ALIGN
    src = jnp.sum(jnp.where(in_run, row, 0), axis=-1)
    assert n_merge * SPARE_BLOCKS >= 2 * BLK_PER_TILE
    k = (tile_id[:, None] % 2) * BLK_PER_TILE + slot
    trash = (k // SPARE_BLOCKS) * SORT_ROWS + SORT_USED + (k % SPARE_BLOCKS) * RUN_ALIGN
    dst = jnp.where(valid, src, trash).astype(jnp.int32).reshape(-1)
    src = jnp.where(valid, src, jnp.where(valid[:, 0:1], src[:, 0:1], 0)).astype(jnp.int32).reshape(-1)
    return tile_e, n_used.astype(jnp.int32).reshape(1), src, dst


def _expert_kernel(tile_e_ref, n_used_ref, src_ref, dst_ref,
                   hs_ref, w1_ref, w3_ref, w2_ref, ys_ref,
                   xbuf, ybuf, w1b, w3b, w2b, gsem, ssem):
    i = pl.program_id(0)
    n_used = n_used_ref[0]
    slot = i % 2

    def gather_copy(tile, s, m):
        src = pl.multiple_of(src_ref[tile * BLK_PER_TILE + m], RUN_ALIGN)
        return pltpu.make_async_copy(hs_ref.at[pl.ds(src, RUN_ALIGN), :],
                                     xbuf.at[s, pl.ds(m * RUN_ALIGN, RUN_ALIGN), :], gsem.at[s])

    def scatter_copy(tile, s, m):
        dst = pl.multiple_of(dst_ref[tile * BLK_PER_TILE + m], RUN_ALIGN)
        return pltpu.make_async_copy(ybuf.at[s, pl.ds(m * RUN_ALIGN, RUN_ALIGN), :],
                                     ys_ref.at[pl.ds(dst, RUN_ALIGN), :], ssem.at[s])

    def start_all(copy, tile, s):
        for m in range(BLK_PER_TILE):
            copy(tile, s, m).start()

    def wait_all(copy, tile, s):
        for m in range(BLK_PER_TILE):
            copy(tile, s, m).wait()

    @pl.when(i == 0)
    def _():
        start_all(gather_copy, 0, 0)

    @pl.when(i < n_used)
    def _():
        e = tile_e_ref[i]
        e_prev = tile_e_ref[jnp.maximum(i - 1, 0)]

        @pl.when((i == 0) | (e != e_prev))
        def _():
            w1b[...] = w1_ref[...].astype(BF16)
            w3b[...] = w3_ref[...].astype(BF16)
            w2b[...] = w2_ref[...].astype(BF16)

        wait_all(gather_copy, i, slot)

        @pl.when(i >= 2)
        def _():
            wait_all(scatter_copy, i - 2, slot)

        nxt = jnp.minimum(i + 1, n_used - 1)
        start_all(gather_copy, nxt, 1 - slot)
        x = xbuf[slot]
        a = jnp.dot(x, w1b[...], preferred_element_type=F32)
        b = jnp.dot(x, w3b[...], preferred_element_type=F32)
        hid = (a * jax.nn.sigmoid(a) * b).astype(BF16)
        ybuf[slot] = jnp.dot(hid, w2b[...], preferred_element_type=F32).astype(BF16)

        @pl.when(i == n_used - 1)
        def _():
            wait_all(gather_copy, nxt, 1 - slot)

        start_all(scatter_copy, i, slot)

        @pl.when(i == n_used - 1)
        def _():
            @pl.when(i >= 1)
            def _():
                wait_all(scatter_copy, i - 1, 1 - slot)
            wait_all(scatter_copy, i, slot)


def _experts(hs, cnt, w1, w3, w2, layer):
    n_rows, d = hs.shape
    f = w1.shape[-1]
    first_expert = layer * N_EXPERTS
    n_tiles_max = n_rows // TR_MOE + N_EXPERTS
    tile_e, n_used, src, dst = _dispatch_plan(cnt, n_tiles_max)
    w_in_map = lambda i, te, nu, s, dd: (first_expert + te[i], 0, 0)
    grid_spec = pltpu.PrefetchScalarGridSpec(
        num_scalar_prefetch=4,
        grid=(n_tiles_max,),
        in_specs=[
            pl.BlockSpec(memory_space=pl.ANY),
            pl.BlockSpec((None, d, f), w_in_map),
            pl.BlockSpec((None, d, f), w_in_map),
            pl.BlockSpec((None, f, d), w_in_map),
        ],
        out_specs=pl.BlockSpec(memory_space=pl.ANY),
        scratch_shapes=[
            pltpu.VMEM((2, TR_MOE, d), BF16),
            pltpu.VMEM((2, TR_MOE, d), BF16),
            pltpu.VMEM((d, f), BF16),
            pltpu.VMEM((d, f), BF16),
            pltpu.VMEM((f, d), BF16),
            pltpu.SemaphoreType.DMA((2,)),
            pltpu.SemaphoreType.DMA((2,)),
        ],
    )
    return pl.pallas_call(
        _expert_kernel,
        grid_spec=grid_spec,
        out_shape=jax.ShapeDtypeStruct(hs.shape, hs.dtype),
        input_output_aliases={4: 0},
        compiler_params=_params("arbitrary"),
        name="experts",
    )(tile_e, n_used, src, dst, hs, w1, w3, w2)


def _combine_kernel(ys_ref, meta_ref, x1_ref, mod_ref, o_ref):
    tm = x1_ref.shape[0]
    d = D_MODEL
    meta = meta_ref[...]
    r_id = lax.broadcasted_iota(jnp.int32, (tm, SORT_ROWS), 1).astype(F32)
    ys = ys_ref[...]
    pick1 = jnp.where(r_id == meta[:, 0:1], 1.0, 0.0).astype(BF16)
    pick2 = jnp.where(r_id == meta[:, 1:2], 1.0, 0.0).astype(BF16)
    y = (meta[:, 2:3] * jnp.dot(pick1, ys, preferred_element_type=F32)
         + meta[:, 3:4] * jnp.dot(pick2, ys, preferred_element_type=F32))
    o_ref[...] = x1_ref[...] + mod_ref[:, 5 * d:6 * d] * y


def _combine(ys, meta, x1, mod, seq):
    t, d = x1.shape
    tm = TM_MERGE
    tiles_per_seq = seq // tm
    return pl.pallas_call(
        _combine_kernel,
        grid=(t // tm,),
        in_specs=[
            pl.BlockSpec((SORT_ROWS, d), lambda i: (i, 0)),
            pl.BlockSpec((tm, LANES), lambda i: (i, 0)),
            pl.BlockSpec((tm, d), lambda i: (i, 0)),
            pl.BlockSpec((None, 1, N_ADA * d), lambda i: (i // tiles_per_seq, 0, 0)),
        ],
        out_specs=pl.BlockSpec((tm, d), lambda i: (i, 0)),
        out_shape=jax.ShapeDtypeStruct((t, d), F32),
        compiler_params=_params("parallel"),
        name="combine",
    )(ys, meta, x1, mod)


@jax.jit
def kernel(x, c, w_ada, b_ada, norm1, w_in, q_norm, k_norm, w_attn_up, w_pool, pool_scale, w_out,
           norm2, w_router_g, b_router_g, w_router_e, b_router_e, w1, w3, w2):
    bsz, seq, d = x.shape
    n_layers = w_ada.shape[0]
    t = bsz * seq
    mod_all = _ada(c, w_ada, b_ada).reshape(n_layers, bsz, 1, N_ADA * d)

    pad = LANES - N_EXPERTS - N_EXPERT_GROUPS
    w_router = jnp.concatenate(
        [w_router_e, w_router_g, jnp.zeros((n_layers, d, pad), F32)], axis=-1)
    b_router = jnp.concatenate(
        [b_router_e, b_router_g, jnp.zeros((n_layers, pad), F32)], axis=-1).reshape(n_layers, 1, LANES)
    heads_per_block = LANES // HEAD_DIM
    q_gain = jnp.tile(q_norm, (1, heads_per_block)).reshape(n_layers, 1, LANES)
    k_gain = jnp.tile(k_norm, (1, heads_per_block)).reshape(n_layers, 1, LANES)

    w1_all = w1.reshape(n_layers * N_EXPERTS, d, EXPERT_HIDDEN)
    w3_all = w3.reshape(n_layers * N_EXPERTS, d, EXPERT_HIDDEN)
    w2_all = w2.reshape(n_layers * N_EXPERTS, EXPERT_HIDDEN, d)

    xt = x.reshape(t, d)
    for l in range(n_layers):
        mod = mod_all[l]
        qkv, pin, gates = _in_proj(xt, mod, norm1[l].reshape(1, d), w_in[l].astype(BF16), seq)
        attn = _attention(qkv, q_gain[l], k_gain[l], seq)
        x1, hs, meta, cnt = _merge(xt, attn, pin, gates, mod, w_attn_up[l].astype(BF16),
                                   w_pool[l].astype(BF16), pool_scale[l].reshape(1, d),
                                   w_out[l].astype(BF16), norm2[l].reshape(1, d), w_router[l], b_router[l], seq)
        cnt = cnt[:, 0, :N_EXPERTS].astype(jnp.int32)
        ys = _experts(hs, cnt, w1_all, w3_all, w2_all, l)
        xt = _combine(ys, meta, x1, mod, seq)
    return xt.reshape(bsz, seq, d)
```

```python
import functools

import jax
import jax.numpy as jnp
from jax import lax
from jax.experimental import pallas as pl
from jax.experimental.pallas import tpu as pltpu

F32 = jnp.float32
BF16 = jnp.bfloat16

D_MODEL = 1024
N_HEADS = 8
HEAD_DIM = 64
ATTN_WIDTH = N_HEADS * HEAD_DIM
MOBA_BLOCK = 256
MOBA_TOPK = 3
POOL_WINDOWS = (2, 4, 8, 16)
N_POOL_GROUPS = len(POOL_WINDOWS)
POOL_WIDTH = D_MODEL // 2
POOL_GROUP_IN = POOL_WIDTH // N_POOL_GROUPS
POOL_GROUP_OUT = D_MODEL // N_POOL_GROUPS
N_EXPERT_GROUPS = 4
EXPERTS_PER_GROUP = 8
N_EXPERTS = N_EXPERT_GROUPS * EXPERTS_PER_GROUP
EXPERT_HIDDEN = 256
N_ADA = 6
EPS = 1e-6
LOG2_E = 1.4426950408889634
MASK_BIAS = -1e30

LANES = 128
POOL_HALO = 16
QKV_WIDTH = 3 * ATTN_WIDTH
GATE_OFF = QKV_WIDTH + POOL_WIDTH
VMEM_LIMIT = 56 * 1024 * 1024

TM_IN = 512
TM_MERGE = 512
TN_ADA = 1536
RUN_ALIGN = 16
SORT_USED = 2 * TM_MERGE + N_EXPERTS * (RUN_ALIGN - 1)
SPARE_BLOCKS = 2
SORT_ROWS = SORT_USED + SPARE_BLOCKS * RUN_ALIGN
TR_MOE = 512
BLK_PER_TILE = TR_MOE // RUN_ALIGN


def _params(*sem):
    return pltpu.CompilerParams(dimension_semantics=sem, vmem_limit_bytes=VMEM_LIMIT)


def _ada_kernel(c_ref, w_ref, b_ref, o_ref):
    c = c_ref[...]
    c_act = (c * jax.nn.sigmoid(c)).astype(BF16)
    o_ref[...] = jnp.dot(c_act, w_ref[...].astype(BF16), preferred_element_type=F32) + b_ref[...]


def _ada(c, w_ada, b_ada):
    n_layers, d, n = w_ada.shape
    b = c.shape[0]
    return pl.pallas_call(
        _ada_kernel,
        grid=(n_layers, n // TN_ADA),
        in_specs=[
            pl.BlockSpec((b, d), lambda l, j: (0, 0)),
            pl.BlockSpec((None, d, TN_ADA), lambda l, j: (l, 0, j)),
            pl.BlockSpec((None, 1, TN_ADA), lambda l, j: (l, 0, j)),
        ],
        out_specs=pl.BlockSpec((None, b, TN_ADA), lambda l, j: (l, 0, j)),
        out_shape=jax.ShapeDtypeStruct((n_layers, b, n), F32),
        compiler_params=_params("parallel", "parallel"),
        name="ada",
    )(c, w_ada, b_ada.reshape(n_layers, 1, n))


def _modulated_norm(x, gain, shift, scale):
    ms = jnp.mean(x * x, axis=-1, keepdims=True)
    xn = x * lax.rsqrt(ms + EPS) * gain
    return xn * (1.0 + scale) + shift


def _cast_once(src_ref, dst_ref, chunk):
    @pl.when(pl.program_id(0) == 0)
    def _():
        for c in range(0, src_ref.shape[-1], chunk):
            dst_ref[..., c:c + chunk] = src_ref[..., c:c + chunk].astype(BF16)


def _in_kernel(x_ref, mod_ref, g_ref, w32_ref, qkv_ref, pin_ref, gates_ref, w_ref):
    _cast_once(w32_ref, w_ref, POOL_WIDTH)
    h = _modulated_norm(x_ref[...], g_ref[...], mod_ref[:, 0:D_MODEL],
                        mod_ref[:, D_MODEL:2 * D_MODEL]).astype(BF16)
    qkv_ref[...] = jnp.dot(h, w_ref[:, 0:QKV_WIDTH], preferred_element_type=F32).astype(BF16)
    pin_ref[...] = jnp.dot(h, w_ref[:, QKV_WIDTH:GATE_OFF], preferred_element_type=F32).astype(BF16)
    gates_ref[...] = jnp.dot(h, w_ref[:, GATE_OFF:], preferred_element_type=F32).astype(BF16)


def _resident(shape, layer):
    zeros = (0,) * len(shape)
    return pl.BlockSpec((None,) + shape, lambda i: (layer,) + zeros, pipeline_mode=pl.Buffered(1))


def _in_proj(x, mod, gain, w_in, layer, seq):
    t, d = x.shape
    n = w_in.shape[-1]
    tiles_per_seq = seq // TM_IN
    return pl.pallas_call(
        _in_kernel,
        grid=(t // TM_IN,),
        in_specs=[
            pl.BlockSpec((TM_IN, d), lambda i: (i, 0)),
            pl.BlockSpec((None, 1, N_ADA * d), lambda i: (i // tiles_per_seq, 0, 0)),
            pl.BlockSpec((1, d), lambda i: (0, 0)),
            _resident((d, n), layer),
        ],
        out_specs=[
            pl.BlockSpec((TM_IN, QKV_WIDTH), lambda i: (i, 0)),
            pl.BlockSpec((TM_IN, POOL_WIDTH), lambda i: (i, 0)),
            pl.BlockSpec((TM_IN, 2 * d), lambda i: (i, 0)),
        ],
        out_shape=[
            jax.ShapeDtypeStruct((t, QKV_WIDTH), BF16),
            jax.ShapeDtypeStruct((t, POOL_WIDTH), BF16),
            jax.ShapeDtypeStruct((t, 2 * d), BF16),
        ],
        scratch_shapes=[pltpu.VMEM((d, n), BF16)],
        compiler_params=_params("arbitrary"),
        name="in_proj",
    )(x, mod, gain, w_in)


def _att_kernel(q_ref, k_ref, v_ref, qg_ref, kg_ref, o_ref,
                ka_ref, kb_ref, qa_ref, qb_ref, va_ref, vb_ref, km_ref, *, seq):
    nb = seq // MOBA_BLOCK
    lane = lax.broadcasted_iota(jnp.int32, (1, LANES), 1)
    is_a = lane < HEAD_DIM
    m_a = is_a.astype(F32)
    m_b = 1.0 - m_a
    nt = (((1,), (1,)), ((), ()))
    neg_inf = -jnp.inf

    def head_norm(z, gain):
        z2 = z * z
        ss_a = jnp.sum(z2 * m_a, axis=-1, keepdims=True)
        ss_b = jnp.sum(z2 * m_b, axis=-1, keepdims=True)
        r = jnp.where(is_a, lax.rsqrt(ss_a * (1.0 / HEAD_DIM) + EPS), lax.rsqrt(ss_b * (1.0 / HEAD_DIM) + EPS))
        return z * (r * gain)

    for j in range(nb):
        rows = pl.ds(j * MOBA_BLOCK, MOBA_BLOCK)
        kn = head_norm(k_ref[rows, :].astype(F32), kg_ref[...])
        km_ref[j:j + 1, :] = jnp.mean(kn, axis=0, keepdims=True)
        ka_ref[rows, :] = jnp.where(is_a, kn, (lane == HEAD_DIM + j).astype(F32)).astype(BF16)
        kb_ref[rows, :] = jnp.where(is_a, (lane == j).astype(F32), kn).astype(BF16)
        v = v_ref[rows, :]
        va_ref[rows, :] = jnp.where(is_a, v, (lane == HEAD_DIM).astype(BF16))
        vb_ref[rows, :] = jnp.where(is_a, (lane == 0).astype(BF16), v)

    km = km_ref[...]
    km_both = jnp.concatenate([km * m_a, km * m_b], axis=0)
    blk = lax.broadcasted_iota(jnp.int32, (nb, 1), 0)
    row_id = lax.broadcasted_iota(jnp.int32, (MOBA_BLOCK, MOBA_BLOCK), 0)
    col_id = lax.broadcasted_iota(jnp.int32, (MOBA_BLOCK, MOBA_BLOCK), 1)
    causal = col_id <= row_id
    eye = jnp.where(col_id == row_id, 1.0, 0.0).astype(BF16)

    qn_all = head_norm(q_ref[...].astype(F32), qg_ref[...])
    gates = lax.dot_general(km_both, qn_all, nt, precision=lax.Precision.HIGHEST,
                            preferred_element_type=F32)
    own = lax.broadcasted_iota(jnp.int32, (1, seq), 1) // MOBA_BLOCK
    bias_t = []
    for h in range(2):
        gate = jnp.where(blk < own, gates[h * nb:(h + 1) * nb, :], neg_inf)
        rank = jnp.zeros(gate.shape, jnp.int32)
        for l in range(nb - 1):
            g_l = gate[l:l + 1, :]
            ahead = (g_l > gate) | ((g_l == gate) & (l < blk))
            rank = rank + ahead.astype(jnp.int32)
        attended = ((rank < MOBA_TOPK) & (blk < own)) | (blk == own)
        bias_t.append(jnp.where(attended, 0.0, MASK_BIAS))
    pad = jnp.zeros((HEAD_DIM - nb, seq), F32)
    bias_t = jnp.concatenate([bias_t[1], pad, bias_t[0], pad], axis=0).astype(BF16)
    for i in range(nb):
        qrows = pl.ds(i * MOBA_BLOCK, MOBA_BLOCK)
        bias = lax.dot_general(eye, bias_t[:, i * MOBA_BLOCK:(i + 1) * MOBA_BLOCK], nt,
                               preferred_element_type=F32)
        q_aug = qn_all[i * MOBA_BLOCK:(i + 1) * MOBA_BLOCK] * (HEAD_DIM ** -0.5 * LOG2_E)
        qa_ref[qrows, :] = jnp.where(is_a, q_aug, bias).astype(BF16)
        qb_ref[qrows, :] = jnp.where(is_a, bias, q_aug).astype(BF16)

    for i in reversed(range(nb)):
        qrows = pl.ds(i * MOBA_BLOCK, MOBA_BLOCK)
        outs = []
        for q_ref_h, k_ref_h, v_ref_h, sum_lane in ((qa_ref, ka_ref, va_ref, HEAD_DIM),
                                                   (qb_ref, kb_ref, vb_ref, 0)):
            qh = q_ref_h[qrows, :]
            s_blocks = []
            m_run = None
            for j in range(i + 1):
                krows = pl.ds(j * MOBA_BLOCK, MOBA_BLOCK)
                s = lax.dot_general(qh, k_ref_h[krows, :], nt, preferred_element_type=F32)
                if j == i:
                    s = jnp.where(causal, s, neg_inf)
                s_blocks.append(s)
                m_blk = jnp.maximum(s[:, :LANES], s[:, LANES:])
                m_run = m_blk if m_run is None else jnp.maximum(m_run, m_blk)
            m_row = jnp.max(m_run, axis=-1, keepdims=True)
            p_blocks = [jnp.exp2(s - m_row).astype(BF16) for s in s_blocks]
            p_all = p_blocks[0] if i == 0 else jnp.concatenate(p_blocks, axis=-1)
            acc = jnp.dot(p_all, v_ref_h[0:(i + 1) * MOBA_BLOCK, :], preferred_element_type=F32)
            outs.append(acc * (1.0 / acc[:, sum_lane:sum_lane + 1]))
        o_ref[qrows, :] = jnp.where(is_a, outs[0], outs[1]).astype(o_ref.dtype)


def _attention(qkv, q_gain, k_gain, seq):
    t = qkv.shape[0]
    n_pairs = ATTN_WIDTH // LANES
    kern = functools.partial(_att_kernel, seq=seq)
    return pl.pallas_call(
        kern,
        grid=(t // seq, n_pairs),
        in_specs=[
            pl.BlockSpec((seq, LANES), lambda b, h: (b, h)),
            pl.BlockSpec((seq, LANES), lambda b, h: (b, n_pairs + h)),
            pl.BlockSpec((seq, LANES), lambda b, h: (b, 2 * n_pairs + h)),
            pl.BlockSpec((1, LANES), lambda b, h: (0, 0)),
            pl.BlockSpec((1, LANES), lambda b, h: (0, 0)),
        ],
        out_specs=pl.BlockSpec((seq, LANES), lambda b, h: (b, h)),
        out_shape=jax.ShapeDtypeStruct((t, ATTN_WIDTH), BF16),
        scratch_shapes=[
            *[pltpu.VMEM((seq, LANES), BF16)] * 6,
            pltpu.VMEM((seq // MOBA_BLOCK, LANES), F32),
        ],
        compiler_params=_params("parallel", "parallel"),
        name="moba_attention",
    )(qkv, qkv, qkv, q_gain, k_gain)


def _merge_kernel(x_ref, attn_ref, pin_ref, halo_ref, gates_ref, mod_ref, wup32_ref, wpool32_ref,
                  pscale_ref, wout32_ref, g2_ref, wr_ref, br_ref,
                  x1_ref, hs_ref, meta_ref, cnt_ref, pool_ref, wup_ref, wpool_ref, wout_ref, *, seq):
    tm = x_ref.shape[0]
    d = D_MODEL
    _cast_once(wup32_ref, wup_ref, d)
    _cast_once(wpool32_ref, wpool_ref, POOL_GROUP_OUT)
    _cast_once(wout32_ref, wout_ref, d)
    seq_pos = (pl.program_id(0) * tm) % seq

    halo = halo_ref[...].astype(F32)
    pool_ref[0:POOL_HALO, :] = jnp.where(seq_pos == 0, 0.0, halo)
    pool_ref[POOL_HALO:, :] = pin_ref[...].astype(F32)
    pos = seq_pos + lax.broadcasted_iota(jnp.int32, (tm, 1), 0)

    a_up = jnp.dot(attn_ref[...], wup_ref[...], preferred_element_type=F32)
    merged = []
    for g, w in enumerate(POOL_WINDOWS):
        cols = slice(g * POOL_GROUP_IN, (g + 1) * POOL_GROUP_IN)
        tok = pool_ref[POOL_HALO:, cols]
        win = tok
        for k in range(1, w):
            win = win + pool_ref[POOL_HALO - k:POOL_HALO - k + tm, cols]
        cnt = jnp.minimum(pos + 1, w).astype(F32)
        pooled = (win / cnt - tok).astype(BF16)
        ocols = slice(g * POOL_GROUP_OUT, (g + 1) * POOL_GROUP_OUT)
        b_pool = jnp.dot(pooled, wpool_ref[g], preferred_element_type=F32) * pscale_ref[:, ocols]
        ga = gates_ref[:, ocols].astype(F32)
        gp = gates_ref[:, d + g * POOL_GROUP_OUT:d + (g + 1) * POOL_GROUP_OUT].astype(F32)
        merged.append((jax.nn.sigmoid(ga) * a_up[:, ocols] + jax.nn.sigmoid(gp) * b_pool).astype(BF16))
    merged = jnp.concatenate(merged, axis=-1)
    y = jnp.dot(merged, wout_ref[...], preferred_element_type=F32)
    x1 = x_ref[...] + mod_ref[:, 2 * d:3 * d] * y
    x1_ref[...] = x1

    h2 = _modulated_norm(x1, g2_ref[...], mod_ref[:, 3 * d:4 * d], mod_ref[:, 4 * d:5 * d])

    h2_hi = h2.astype(BF16)
    h2_lo = (h2 - h2_hi.astype(F32)).astype(BF16)
    wr = wr_ref[...]
    wr_hi = wr.astype(BF16)
    wr_lo = (wr - wr_hi.astype(F32)).astype(BF16)
    logits = (jnp.dot(h2_hi, wr_hi, preferred_element_type=F32)
              + jnp.dot(h2_lo, wr_hi, preferred_element_type=F32)
              + jnp.dot(h2_hi, wr_lo, preferred_element_type=F32)) + br_ref[...]
    lane = lax.broadcasted_iota(jnp.int32, (1, LANES), 1)
    neg_inf = -jnp.inf
    is_group = (lane >= N_EXPERTS) & (lane < N_EXPERTS + N_EXPERT_GROUPS)
    gl = jnp.where(is_group, logits, neg_inf)
    g_max = jnp.max(gl, axis=-1, keepdims=True)
    g_top = 1.0 / jnp.sum(jnp.exp(gl - g_max), axis=-1, keepdims=True)
    g_idx = jnp.min(jnp.where(gl == g_max, lane, LANES), axis=-1, keepdims=True) - N_EXPERTS
    in_group = (lane < N_EXPERTS) & ((lane // EXPERTS_PER_GROUP) == g_idx)
    el = jnp.where(in_group, logits, neg_inf)
    e1 = jnp.max(el, axis=-1, keepdims=True)
    i1 = jnp.min(jnp.where(el == e1, lane, LANES), axis=-1, keepdims=True)
    el2 = jnp.where(lane == i1, neg_inf, el)
    e2 = jnp.max(el2, axis=-1, keepdims=True)
    i2 = jnp.min(jnp.where(el2 == e2, lane, LANES), axis=-1, keepdims=True)
    r = jnp.exp(e2 - e1)
    w_first = g_top / (1.0 + r)
    w_second = w_first * r

    onehot1 = jnp.where(lane == i1, 1.0, 0.0)
    onehot2 = jnp.where(lane == i2, 1.0, 0.0)
    t_row = lax.broadcasted_iota(jnp.int32, (tm, tm), 0)
    t_col = lax.broadcasted_iota(jnp.int32, (tm, tm), 1)
    earlier = jnp.where(t_col < t_row, 1.0, 0.0).astype(BF16)
    both = jnp.concatenate([onehot1, onehot2], axis=-1).astype(BF16)
    before = jnp.dot(earlier, both, preferred_element_type=F32)
    before1, before2 = before[:, :LANES], before[:, LANES:]
    cnt1 = jnp.sum(onehot1, axis=0, keepdims=True)
    cnt = cnt1 + jnp.sum(onehot2, axis=0, keepdims=True)
    run_len = ((cnt.astype(jnp.int32) + (RUN_ALIGN - 1)) // RUN_ALIGN * RUN_ALIGN).astype(F32)
    e_row = lax.broadcasted_iota(jnp.int32, (LANES, LANES), 0)
    e_col = lax.broadcasted_iota(jnp.int32, (LANES, LANES), 1)
    lower_experts = jnp.where(e_row < e_col, 1.0, 0.0)
    run_start = jnp.dot(jnp.broadcast_to(run_len, (8, LANES)), lower_experts,
                        precision=lax.Precision.HIGHEST, preferred_element_type=F32)[0:1, :]
    pos1 = jnp.sum(onehot1 * (run_start + before1), axis=-1, keepdims=True)
    pos2 = jnp.sum(onehot2 * (run_start + cnt1 + before2), axis=-1, keepdims=True)
    meta = jnp.where(lane == 0, pos1, jnp.where(lane == 1, pos2,
                     jnp.where(lane == 2, w_first, jnp.where(lane == 3, w_second, 0.0))))
    meta_ref[...] = meta
    cnt_ref[...] = jnp.broadcast_to(cnt, (8, LANES))

    eye8 = jnp.where(lax.broadcasted_iota(jnp.int32, (8, LANES), 0)
                     == lax.broadcasted_iota(jnp.int32, (8, LANES), 1), 1.0, 0.0)
    pos_rows = lax.dot_general(eye8, meta, (((1,), (1,)), ((), ())),
                               precision=lax.Precision.HIGHEST, preferred_element_type=F32)
    r_id = lax.broadcasted_iota(jnp.int32, (SORT_ROWS, tm), 0).astype(F32)
    hit = (r_id == pos_rows[0:1, :]) | (r_id == pos_rows[1:2, :])
    perm = jnp.where(hit, 1.0, 0.0).astype(BF16)
    hs_ref[...] = jnp.dot(perm, h2_hi, preferred_element_type=F32).astype(BF16)


def _merge(x, attn, pin, gates, mod, w_up, w_pool, pool_scale, w_out, gain2, w_router, b_router, layer, seq):
    t, d = x.shape
    tm = TM_MERGE
    n_tiles = t // tm
    tiles_per_seq = seq // tm
    halo_per_tile = tm // POOL_HALO
    kern = functools.partial(_merge_kernel, seq=seq)
    const2 = lambda i: (0, 0)
    return pl.pallas_call(
        kern,
        grid=(t // tm,),
        in_specs=[
            pl.BlockSpec((tm, d), lambda i: (i, 0)),
            pl.BlockSpec((tm, ATTN_WIDTH), lambda i: (i, 0)),
            pl.BlockSpec((tm, POOL_WIDTH), lambda i: (i, 0)),
            pl.BlockSpec((POOL_HALO, POOL_WIDTH), lambda i: (jnp.maximum(i * halo_per_tile - 1, 0), 0)),
            pl.BlockSpec((tm, 2 * d), lambda i: (i, 0)),
            pl.BlockSpec((None, 1, N_ADA * d), lambda i: (i // tiles_per_seq, 0, 0)),
            _resident((ATTN_WIDTH, d), layer),
            _resident((N_POOL_GROUPS, POOL_GROUP_IN, POOL_GROUP_OUT), layer),
            pl.BlockSpec((1, d), const2),
            _resident((d, d), layer),
            pl.BlockSpec((1, d), const2),
            pl.BlockSpec((d, LANES), const2),
            pl.BlockSpec((1, LANES), const2),
        ],
        out_specs=[
            pl.BlockSpec((tm, d), lambda i: (i, 0)),
            pl.BlockSpec((SORT_ROWS, d), lambda i: (i, 0)),
            pl.BlockSpec((tm, LANES), lambda i: (i, 0)),
            pl.BlockSpec((None, 8, LANES), lambda i: (i, 0, 0)),
        ],
        out_shape=[
            jax.ShapeDtypeStruct((t, d), F32),
            jax.ShapeDtypeStruct((n_tiles * SORT_ROWS, d), BF16),
            jax.ShapeDtypeStruct((t, LANES), F32),
            jax.ShapeDtypeStruct((n_tiles, 8, LANES), F32),
        ],
        scratch_shapes=[
            pltpu.VMEM((tm + POOL_HALO, POOL_WIDTH), F32),
            pltpu.VMEM((ATTN_WIDTH, d), BF16),
            pltpu.VMEM((N_POOL_GROUPS, POOL_GROUP_IN, POOL_GROUP_OUT), BF16),
            pltpu.VMEM((d, d), BF16),
        ],
        compiler_params=_params("arbitrary"),
        name="merge_router",
    )(x, attn, pin, pin, gates, mod, w_up, w_pool, pool_scale, w_out, gain2, w_router, b_router)


def _dispatch_plan(cnt, n_tiles_max):
    n_merge = cnt.shape[0]
    i32 = jnp.int32

    def before(n):
        return (jnp.arange(n, dtype=i32)[:, None] < jnp.arange(n, dtype=i32)[None, :]).astype(i32)

    run_len = (cnt + (RUN_ALIGN - 1)) // RUN_ALIGN * RUN_ALIGN
    run_start = jnp.sum(run_len[:, :, None] * before(N_EXPERTS)[None], axis=1)
    run_blocks = (run_len // RUN_ALIGN).T
    run_first = jnp.sum(run_blocks[:, :, None] * before(n_merge)[None], axis=1)
    blocks_e = jnp.sum(run_blocks, axis=1)
    tiles_e = (blocks_e + BLK_PER_TILE - 1) // BLK_PER_TILE
    tiles_first = jnp.sum(tiles_e[:, None] * before(N_EXPERTS), axis=0)
    n_used = jnp.sum(tiles_e)
    tile_id = jnp.arange(n_tiles_max, dtype=i32)
    tile_e = jnp.sum(((tiles_first + tiles_e)[None, :] <= tile_id[:, None]).astype(i32), axis=1)
    tile_e = jnp.minimum(tile_e, N_EXPERTS - 1)
    of_tile = (tile_e[:, None] == jnp.arange(N_EXPERTS, dtype=i32)[None, :]).astype(i32)

    def per_tile(table):
        return jnp.sum(of_tile[:, :, None] * table[None], axis=1)

    tile_first = jnp.sum(of_tile * tiles_first[None, :], axis=1)
    tile_blocks = jnp.sum(of_tile * blocks_e[None, :], axis=1)
    first_t, nblk_t, start_t = per_tile(run_first), per_tile(run_blocks), per_tile(run_start.T)

    slot = jnp.arange(BLK_PER_TILE, dtype=i32)[None, :]
    blk = (tile_id - tile_first)[:, None] * BLK_PER_TILE + slot
    valid = (tile_id[:, None] < n_used) & (blk < tile_blocks[:, None])
    off = blk[:, :, None] - first_t[:, None, :]
    in_run = (off >= 0) & (off < nblk_t[:, None, :])
    row = jnp.arange(n_merge, dtype=i32) * SORT_ROWS + start_t[:, None, :] + off * RUN_ALIGN
    src = jnp.sum(jnp.where(in_run, row, 0), axis=-1)
    assert n_merge * SPARE_BLOCKS >= 2 * BLK_PER_TILE
    k = (tile_id[:, None] % 2) * BLK_PER_TILE + slot
    trash = (k // SPARE_BLOCKS) * SORT_ROWS + SORT_USED + (k % SPARE_BLOCKS) * RUN_ALIGN
    dst = jnp.where(valid, src, trash).astype(jnp.int32).reshape(-1)
    src = jnp.where(valid, src, jnp.where(valid[:, 0:1], src[:, 0:1], 0)).astype(jnp.int32).reshape(-1)
    return tile_e, n_used.astype(jnp.int32).reshape(1), src, dst


def _expert_kernel(tile_e_ref, n_used_ref, src_ref, dst_ref,
                   hs_ref, w1_ref, w3_ref, w2_ref, ys_ref,
                   xbuf, ybuf, w1b, w3b, w2b, gsem, ssem):
    i = pl.program_id(0)
    n_used = n_used_ref[0]
    slot = i % 2

    def gather_copy(tile, s, m):
        src = pl.multiple_of(src_ref[tile * BLK_PER_TILE + m], RUN_ALIGN)
        return pltpu.make_async_copy(hs_ref.at[pl.ds(src, RUN_ALIGN), :],
                                     xbuf.at[s, pl.ds(m * RUN_ALIGN, RUN_ALIGN), :], gsem.at[s])

    def scatter_copy(tile, s, m):
        dst = pl.multiple_of(dst_ref[tile * BLK_PER_TILE + m], RUN_ALIGN)
        return pltpu.make_async_copy(ybuf.at[s, pl.ds(m * RUN_ALIGN, RUN_ALIGN), :],
                                     ys_ref.at[pl.ds(dst, RUN_ALIGN), :], ssem.at[s])

    def start_all(copy, tile, s):
        for m in range(BLK_PER_TILE):
            copy(tile, s, m).start()

    def wait_all(copy, tile, s):
        for m in range(BLK_PER_TILE):
            copy(tile, s, m).wait()

    @pl.when(i == 0)
    def _():
        start_all(gather_copy, 0, 0)

    @pl.when(i < n_used)
    def _():
        e = tile_e_ref[i]
        e_prev = tile_e_ref[jnp.maximum(i - 1, 0)]

        @pl.when((i == 0) | (e != e_prev))
        def _():
            w1b[...] = w1_ref[...].astype(BF16)
            w3b[...] = w3_ref[...].astype(BF16)
            w2b[...] = w2_ref[...].astype(BF16)

        wait_all(gather_copy, i, slot)

        @pl.when(i >= 2)
        def _():
            wait_all(scatter_copy, i - 2, slot)

        nxt = jnp.minimum(i + 1, n_used - 1)
        start_all(gather_copy, nxt, 1 - slot)
        x = xbuf[slot]
        a = jnp.dot(x, w1b[...], preferred_element_type=F32)
        b = jnp.dot(x, w3b[...], preferred_element_type=F32)
        hid = (a * jax.nn.sigmoid(a) * b).astype(BF16)
        ybuf[slot] = jnp.dot(hid, w2b[...], preferred_element_type=F32).astype(BF16)

        @pl.when(i == n_used - 1)
        def _():
            wait_all(gather_copy, nxt, 1 - slot)

        start_all(scatter_copy, i, slot)

        @pl.when(i == n_used - 1)
        def _():
            @pl.when(i >= 1)
            def _():
                wait_all(scatter_copy, i - 1, 1 - slot)
            wait_all(scatter_copy, i, slot)


def _experts(hs, cnt, w1, w3, w2, layer):
    n_rows, d = hs.shape
    f = w1.shape[-1]
    first_expert = layer * N_EXPERTS
    n_tiles_max = n_rows // TR_MOE + N_EXPERTS
    tile_e, n_used, src, dst = _dispatch_plan(cnt, n_tiles_max)
    w_in_map = lambda i, te, nu, s, dd: (first_expert + te[i], 0, 0)
    grid_spec = pltpu.PrefetchScalarGridSpec(
        num_scalar_prefetch=4,
        grid=(n_tiles_max,),
        in_specs=[
            pl.BlockSpec(memory_space=pl.ANY),
            pl.BlockSpec((None, d, f), w_in_map),
            pl.BlockSpec((None, d, f), w_in_map),
            pl.BlockSpec((None, f, d), w_in_map),
        ],
        out_specs=pl.BlockSpec(memory_space=pl.ANY),
        scratch_shapes=[
            pltpu.VMEM((2, TR_MOE, d), BF16),
            pltpu.VMEM((2, TR_MOE, d), BF16),
            pltpu.VMEM((d, f), BF16),
            pltpu.VMEM((d, f), BF16),
            pltpu.VMEM((f, d), BF16),
            pltpu.SemaphoreType.DMA((2,)),
            pltpu.SemaphoreType.DMA((2,)),
        ],
    )
    return pl.pallas_call(
        _expert_kernel,
        grid_spec=grid_spec,
        out_shape=jax.ShapeDtypeStruct(hs.shape, hs.dtype),
        input_output_aliases={4: 0},
        compiler_params=_params("arbitrary"),
        name="experts",
    )(tile_e, n_used, src, dst, hs, w1, w3, w2)


def _combine_kernel(ys_ref, meta_ref, x1_ref, mod_ref, o_ref):
    tm = x1_ref.shape[0]
    d = D_MODEL
    meta = meta_ref[...]
    r_id = lax.broadcasted_iota(jnp.int32, (tm, SORT_ROWS), 1).astype(F32)
    ys = ys_ref[...]
    pick1 = jnp.where(r_id == meta[:, 0:1], 1.0, 0.0).astype(BF16)
    pick2 = jnp.where(r_id == meta[:, 1:2], 1.0, 0.0).astype(BF16)
    y = (meta[:, 2:3] * jnp.dot(pick1, ys, preferred_element_type=F32)
         + meta[:, 3:4] * jnp.dot(pick2, ys, preferred_element_type=F32))
    o_ref[...] = x1_ref[...] + mod_ref[:, 5 * d:6 * d] * y


def _combine(ys, meta, x1, mod, seq):
    t, d = x1.shape
    tm = TM_MERGE
    tiles_per_seq = seq // tm
    return pl.pallas_call(
        _combine_kernel,
        grid=(t // tm,),
        in_specs=[
            pl.BlockSpec((SORT_ROWS, d), lambda i: (i, 0)),
            pl.BlockSpec((tm, LANES), lambda i: (i, 0)),
            pl.BlockSpec((tm, d), lambda i: (i, 0)),
            pl.BlockSpec((None, 1, N_ADA * d), lambda i: (i // tiles_per_seq, 0, 0)),
        ],
        out_specs=pl.BlockSpec((tm, d), lambda i: (i, 0)),
        out_shape=jax.ShapeDtypeStruct((t, d), F32),
        compiler_params=_params("parallel"),
        name="combine",
    )(ys, meta, x1, mod)


@jax.jit
def kernel(x, c, w_ada, b_ada, norm1, w_in, q_norm, k_norm, w_attn_up, w_pool, pool_scale, w_out,
           norm2, w_router_g, b_router_g, w_router_e, b_router_e, w1, w3, w2):
    bsz, seq, d = x.shape
    n_layers = w_ada.shape[0]
    t = bsz * seq
    mod_all = _ada(c, w_ada, b_ada).reshape(n_layers, bsz, 1, N_ADA * d)

    pad = LANES - N_EXPERTS - N_EXPERT_GROUPS
    w_router = jnp.concatenate(
        [w_router_e, w_router_g, jnp.zeros((n_layers, d, pad), F32)], axis=-1)
    b_router = jnp.concatenate(
        [b_router_e, b_router_g, jnp.zeros((n_layers, pad), F32)], axis=-1).reshape(n_layers, 1, LANES)
    heads_per_block = LANES // HEAD_DIM
    q_gain = jnp.tile(q_norm, (1, heads_per_block)).reshape(n_layers, 1, LANES)
    k_gain = jnp.tile(k_norm, (1, heads_per_block)).reshape(n_layers, 1, LANES)

    w1_all = w1.reshape(n_layers * N_EXPERTS, d, EXPERT_HIDDEN)
    w3_all = w3.reshape(n_layers * N_EXPERTS, d, EXPERT_HIDDEN)
    w2_all = w2.reshape(n_layers * N_EXPERTS, EXPERT_HIDDEN, d)

    xt = x.reshape(t, d)
    for l in range(n_layers):
        mod = mod_all[l]
        qkv, pin, gates = _in_proj(xt, mod, norm1[l].reshape(1, d), w_in, l, seq)
        attn = _attention(qkv, q_gain[l], k_gain[l], seq)
        x1, hs, meta, cnt = _merge(xt, attn, pin, gates, mod, w_attn_up, w_pool, pool_scale[l].reshape(1, d),
                                   w_out, norm2[l].reshape(1, d), w_router[l], b_router[l], l, seq)
        cnt = cnt[:, 0, :N_EXPERTS].astype(jnp.int32)
        ys = _experts(hs, cnt, w1_all, w3_all, w2_all, l)
        xt = _combine(ys, meta, x1, mod, seq)
    return xt.reshape(bsz, seq, d)
```

```python
import functools

import jax
import jax.numpy as jnp
from jax import lax
from jax.experimental import pallas as pl
from jax.experimental.pallas import tpu as pltpu

F32 = jnp.float32
BF16 = jnp.bfloat16

D_MODEL = 1024
N_HEADS = 8
HEAD_DIM = 64
ATTN_WIDTH = N_HEADS * HEAD_DIM
MOBA_BLOCK = 256
MOBA_TOPK = 3
POOL_WINDOWS = (2, 4, 8, 16)
N_POOL_GROUPS = len(POOL_WINDOWS)
POOL_WIDTH = D_MODEL // 2
POOL_GROUP_IN = POOL_WIDTH // N_POOL_GROUPS
POOL_GROUP_OUT = D_MODEL // N_POOL_GROUPS
N_EXPERT_GROUPS = 4
EXPERTS_PER_GROUP = 8
N_EXPERTS = N_EXPERT_GROUPS * EXPERTS_PER_GROUP
EXPERT_HIDDEN = 256
N_ADA = 6
EPS = 1e-6
LOG2_E = 1.4426950408889634
MASK_BIAS = -1e30

LANES = 128
POOL_HALO = 16
QKV_WIDTH = 3 * ATTN_WIDTH
GATE_OFF = QKV_WIDTH + POOL_WIDTH
VMEM_LIMIT = 56 * 1024 * 1024

TM_IN = 512
TM_MERGE = 512
TN_ADA = 1536
RUN_ALIGN = 16
SORT_USED = 2 * TM_MERGE + N_EXPERTS * (RUN_ALIGN - 1)
SPARE_BLOCKS = 2
SORT_ROWS = SORT_USED + SPARE_BLOCKS * RUN_ALIGN
TR_MOE = 512
BLK_PER_TILE = TR_MOE // RUN_ALIGN


def _params(*sem):
    return pltpu.CompilerParams(dimension_semantics=sem, vmem_limit_bytes=VMEM_LIMIT)


def _ada_kernel(c_ref, w_ref, b_ref, o_ref):
    c = c_ref[...]
    c_act = (c * jax.nn.sigmoid(c)).astype(BF16)
    o_ref[...] = jnp.dot(c_act, w_ref[...].astype(BF16), preferred_element_type=F32) + b_ref[...]


def _ada(c, w_ada, b_ada):
    n_layers, d, n = w_ada.shape
    b = c.shape[0]
    return pl.pallas_call(
        _ada_kernel,
        grid=(n_layers, n // TN_ADA),
        in_specs=[
            pl.BlockSpec((b, d), lambda l, j: (0, 0)),
            pl.BlockSpec((None, d, TN_ADA), lambda l, j: (l, 0, j)),
            pl.BlockSpec((None, 1, TN_ADA), lambda l, j: (l, 0, j)),
        ],
        out_specs=pl.BlockSpec((None, b, TN_ADA), lambda l, j: (l, 0, j)),
        out_shape=jax.ShapeDtypeStruct((n_layers, b, n), F32),
        compiler_params=_params("parallel", "parallel"),
        name="ada",
    )(c, w_ada, b_ada.reshape(n_layers, 1, n))


def _modulated_norm(x, gain, shift, scale):
    ms = jnp.mean(x * x, axis=-1, keepdims=True)
    xn = x * lax.rsqrt(ms + EPS) * gain
    return xn * (1.0 + scale) + shift


def _cast_once(src_ref, dst_ref, chunk):
    @pl.when(pl.program_id(0) == 0)
    def _():
        for c in range(0, src_ref.shape[-1], chunk):
            dst_ref[..., c:c + chunk] = src_ref[..., c:c + chunk].astype(BF16)


def _in_kernel(x_ref, mod_ref, g_ref, w32_ref, qkv_ref, pin_ref, gates_ref, w_ref):
    _cast_once(w32_ref, w_ref, POOL_WIDTH)
    h = _modulated_norm(x_ref[...], g_ref[...], mod_ref[:, 0:D_MODEL],
                        mod_ref[:, D_MODEL:2 * D_MODEL]).astype(BF16)
    qkv_ref[...] = jnp.dot(h, w_ref[:, 0:QKV_WIDTH], preferred_element_type=F32).astype(BF16)
    pin_ref[...] = jnp.dot(h, w_ref[:, QKV_WIDTH:GATE_OFF], preferred_element_type=F32).astype(BF16)
    gates_ref[...] = jnp.dot(h, w_ref[:, GATE_OFF:], preferred_element_type=F32).astype(BF16)


def _resident(shape, layer):
    zeros = (0,) * len(shape)
    return pl.BlockSpec((None,) + shape, lambda i: (layer,) + zeros, pipeline_mode=pl.Buffered(1))


def _in_proj(x, mod, gain, w_in, layer, seq):
    t, d = x.shape
    n = w_in.shape[-1]
    tiles_per_seq = seq // TM_IN
    return pl.pallas_call(
        _in_kernel,
        grid=(t // TM_IN,),
        in_specs=[
            pl.BlockSpec((TM_IN, d), lambda i: (i, 0)),
            pl.BlockSpec((None, 1, N_ADA * d), lambda i: (i // tiles_per_seq, 0, 0)),
            pl.BlockSpec((1, d), lambda i: (0, 0)),
            _resident((d, n), layer),
        ],
        out_specs=[
            pl.BlockSpec((TM_IN, QKV_WIDTH), lambda i: (i, 0)),
            pl.BlockSpec((TM_IN, POOL_WIDTH), lambda i: (i, 0)),
            pl.BlockSpec((TM_IN, 2 * d), lambda i: (i, 0)),
        ],
        out_shape=[
            jax.ShapeDtypeStruct((t, QKV_WIDTH), BF16),
            jax.ShapeDtypeStruct((t, POOL_WIDTH), BF16),
            jax.ShapeDtypeStruct((t, 2 * d), BF16),
        ],
        scratch_shapes=[pltpu.VMEM((d, n), BF16)],
        compiler_params=_params("arbitrary"),
        name="in_proj",
    )(x, mod, gain, w_in)


def _att_kernel(q_ref, k_ref, v_ref, qg_ref, kg_ref, o_ref,
                ka_ref, kb_ref, qa_ref, qb_ref, va_ref, vb_ref, km_ref, *, seq):
    nb = seq // MOBA_BLOCK
    lane = lax.broadcasted_iota(jnp.int32, (1, LANES), 1)
    is_a = lane < HEAD_DIM
    m_a = is_a.astype(F32)
    m_b = 1.0 - m_a
    nt = (((1,), (1,)), ((), ()))
    neg_inf = -jnp.inf

    def head_norm(z, gain):
        z2 = z * z
        ss_a = jnp.sum(z2 * m_a, axis=-1, keepdims=True)
        ss_b = jnp.sum(z2 * m_b, axis=-1, keepdims=True)
        r = jnp.where(is_a, lax.rsqrt(ss_a * (1.0 / HEAD_DIM) + EPS), lax.rsqrt(ss_b * (1.0 / HEAD_DIM) + EPS))
        return z * (r * gain)

    for j in range(nb):
        rows = pl.ds(j * MOBA_BLOCK, MOBA_BLOCK)
        kn = head_norm(k_ref[rows, :].astype(F32), kg_ref[...])
        km_ref[j:j + 1, :] = jnp.mean(kn, axis=0, keepdims=True)
        ka_ref[rows, :] = jnp.where(is_a, kn, (lane == HEAD_DIM + j).astype(F32)).astype(BF16)
        kb_ref[rows, :] = jnp.where(is_a, (lane == j).astype(F32), kn).astype(BF16)
        v = v_ref[rows, :]
        va_ref[rows, :] = jnp.where(is_a, v, (lane == HEAD_DIM).astype(BF16))
        vb_ref[rows, :] = jnp.where(is_a, (lane == 0).astype(BF16), v)

    km = km_ref[...]
    km_both = jnp.concatenate([km * m_a, km * m_b], axis=0)
    blk = lax.broadcasted_iota(jnp.int32, (nb, 1), 0)
    row_id = lax.broadcasted_iota(jnp.int32, (MOBA_BLOCK, MOBA_BLOCK), 0)
    col_id = lax.broadcasted_iota(jnp.int32, (MOBA_BLOCK, MOBA_BLOCK), 1)
    causal = col_id <= row_id
    eye = jnp.where(col_id == row_id, 1.0, 0.0).astype(BF16)

    qn_all = head_norm(q_ref[...].astype(F32), qg_ref[...])
    gates = lax.dot_general(km_both, qn_all, nt, precision=lax.Precision.HIGHEST,
                            preferred_element_type=F32)
    own = lax.broadcasted_iota(jnp.int32, (1, seq), 1) // MOBA_BLOCK
    bias_t = []
    for h in range(2):
        gate = jnp.where(blk < own, gates[h * nb:(h + 1) * nb, :], neg_inf)
        rank = jnp.zeros(gate.shape, jnp.int32)
        for l in range(nb - 1):
            g_l = gate[l:l + 1, :]
            ahead = (g_l > gate) | ((g_l == gate) & (l < blk))
            rank = rank + ahead.astype(jnp.int32)
        attended = ((rank < MOBA_TOPK) & (blk < own)) | (blk == own)
        bias_t.append(jnp.where(attended, 0.0, MASK_BIAS))
    pad = jnp.zeros((HEAD_DIM - nb, seq), F32)
    bias_t = jnp.concatenate([bias_t[1], pad, bias_t[0], pad], axis=0).astype(BF16)
    for i in range(nb):
        qrows = pl.ds(i * MOBA_BLOCK, MOBA_BLOCK)
        bias = lax.dot_general(eye, bias_t[:, i * MOBA_BLOCK:(i + 1) * MOBA_BLOCK], nt,
                               preferred_element_type=F32)
        q_aug = qn_all[i * MOBA_BLOCK:(i + 1) * MOBA_BLOCK] * (HEAD_DIM ** -0.5 * LOG2_E)
        qa_ref[qrows, :] = jnp.where(is_a, q_aug, bias).astype(BF16)
        qb_ref[qrows, :] = jnp.where(is_a, bias, q_aug).astype(BF16)

    for i in reversed(range(nb)):
        qrows = pl.ds(i * MOBA_BLOCK, MOBA_BLOCK)
        outs = []
        for q_ref_h, k_ref_h, v_ref_h, sum_lane in ((qa_ref, ka_ref, va_ref, HEAD_DIM),
                                                   (qb_ref, kb_ref, vb_ref, 0)):
            qh = q_ref_h[qrows, :]
            s_blocks = []
            m_run = None
            for j in range(i + 1):
                krows = pl.ds(j * MOBA_BLOCK, MOBA_BLOCK)
                s = lax.dot_general(qh, k_ref_h[krows, :], nt, preferred_element_type=F32)
                if j == i:
                    s = jnp.where(causal, s, neg_inf)
                s_blocks.append(s)
                m_blk = jnp.maximum(s[:, :LANES], s[:, LANES:])
                m_run = m_blk if m_run is None else jnp.maximum(m_run, m_blk)
            m_row = jnp.max(m_run, axis=-1, keepdims=True)
            p_blocks = [jnp.exp2(s - m_row).astype(BF16) for s in s_blocks]
            p_all = p_blocks[0] if i == 0 else jnp.concatenate(p_blocks, axis=-1)
            acc = jnp.dot(p_all, v_ref_h[0:(i + 1) * MOBA_BLOCK, :], preferred_element_type=F32)
            outs.append(acc * (1.0 / acc[:, sum_lane:sum_lane + 1]))
        o_ref[qrows, :] = jnp.where(is_a, outs[0], outs[1]).astype(o_ref.dtype)


def _attention(qkv, q_gain, k_gain, seq):
    t = qkv.shape[0]
    n_pairs = ATTN_WIDTH // LANES
    kern = functools.partial(_att_kernel, seq=seq)
    return pl.pallas_call(
        kern,
        grid=(t // seq, n_pairs),
        in_specs=[
            pl.BlockSpec((seq, LANES), lambda b, h: (b, h)),
            pl.BlockSpec((seq, LANES), lambda b, h: (b, n_pairs + h)),
            pl.BlockSpec((seq, LANES), lambda b, h: (b, 2 * n_pairs + h)),
            pl.BlockSpec((1, LANES), lambda b, h: (0, 0)),
            pl.BlockSpec((1, LANES), lambda b, h: (0, 0)),
        ],
        out_specs=pl.BlockSpec((seq, LANES), lambda b, h: (b, h)),
        out_shape=jax.ShapeDtypeStruct((t, ATTN_WIDTH), BF16),
        scratch_shapes=[
            *[pltpu.VMEM((seq, LANES), BF16)] * 6,
            pltpu.VMEM((seq // MOBA_BLOCK, LANES), F32),
        ],
        compiler_params=_params("parallel", "parallel"),
        name="moba_attention",
    )(qkv, qkv, qkv, q_gain, k_gain)


def _merge_kernel(x_ref, attn_ref, pin_ref, halo_ref, gates_ref, mod_ref, wup32_ref, wpool32_ref,
                  pscale_ref, wout32_ref, g2_ref, wr_ref, br_ref,
                  x1_ref, hs_ref, meta_ref, cnt_ref, pool_ref, wup_ref, wpool_ref, wout_ref, *, seq):
    tm = x_ref.shape[0]
    d = D_MODEL
    _cast_once(wup32_ref, wup_ref, d)
    _cast_once(wpool32_ref, wpool_ref, POOL_GROUP_OUT)
    _cast_once(wout32_ref, wout_ref, d)
    seq_pos = (pl.program_id(0) * tm) % seq

    top = 2 * POOL_HALO
    halo = halo_ref[...].astype(F32)
    pool_ref[0:POOL_HALO, :] = jnp.zeros((POOL_HALO, POOL_WIDTH), F32)
    pool_ref[POOL_HALO:top, :] = jnp.where(seq_pos == 0, 0.0, halo)
    pool_ref[top:, :] = pin_ref[...].astype(F32)
    pos = seq_pos + lax.broadcasted_iota(jnp.int32, (tm, 1), 0)

    a_up = jnp.dot(attn_ref[...], wup_ref[...], preferred_element_type=F32)
    merged = []
    for g, w in enumerate(POOL_WINDOWS):
        cols = slice(g * POOL_GROUP_IN, (g + 1) * POOL_GROUP_IN)
        tok = pool_ref[top:, cols]
        shift = 1
        while shift < w:
            pool_ref[POOL_HALO:, cols] = (pool_ref[POOL_HALO:, cols]
                                         + pool_ref[POOL_HALO - shift:top + tm - shift, cols])
            shift *= 2
        win = pool_ref[top:, cols]
        cnt = jnp.minimum(pos + 1, w).astype(F32)
        pooled = (win / cnt - tok).astype(BF16)
        ocols = slice(g * POOL_GROUP_OUT, (g + 1) * POOL_GROUP_OUT)
        b_pool = jnp.dot(pooled, wpool_ref[g], preferred_element_type=F32) * pscale_ref[:, ocols]
        ga = gates_ref[:, ocols].astype(F32)
        gp = gates_ref[:, d + g * POOL_GROUP_OUT:d + (g + 1) * POOL_GROUP_OUT].astype(F32)
        merged.append((jax.nn.sigmoid(ga) * a_up[:, ocols] + jax.nn.sigmoid(gp) * b_pool).astype(BF16))
    merged = jnp.concatenate(merged, axis=-1)
    y = jnp.dot(merged, wout_ref[...], preferred_element_type=F32)
    x1 = x_ref[...] + mod_ref[:, 2 * d:3 * d] * y
    x1_ref[...] = x1

    h2 = _modulated_norm(x1, g2_ref[...], mod_ref[:, 3 * d:4 * d], mod_ref[:, 4 * d:5 * d])

    h2_hi = h2.astype(BF16)
    h2_lo = (h2 - h2_hi.astype(F32)).astype(BF16)
    wr = wr_ref[...]
    wr_hi = wr.astype(BF16)
    wr_lo = (wr - wr_hi.astype(F32)).astype(BF16)
    logits = (jnp.dot(h2_hi, wr_hi, preferred_element_type=F32)
              + jnp.dot(h2_lo, wr_hi, preferred_element_type=F32)
              + jnp.dot(h2_hi, wr_lo, preferred_element_type=F32)) + br_ref[...]
    lane = lax.broadcasted_iota(jnp.int32, (1, LANES), 1)
    neg_inf = -jnp.inf
    is_group = (lane >= N_EXPERTS) & (lane < N_EXPERTS + N_EXPERT_GROUPS)
    gl = jnp.where(is_group, logits, neg_inf)
    g_max = jnp.max(gl, axis=-1, keepdims=True)
    g_top = 1.0 / jnp.sum(jnp.exp(gl - g_max), axis=-1, keepdims=True)
    g_idx = jnp.min(jnp.where(gl == g_max, lane, LANES), axis=-1, keepdims=True) - N_EXPERTS
    in_group = (lane < N_EXPERTS) & ((lane // EXPERTS_PER_GROUP) == g_idx)
    el = jnp.where(in_group, logits, neg_inf)
    e1 = jnp.max(el, axis=-1, keepdims=True)
    i1 = jnp.min(jnp.where(el == e1, lane, LANES), axis=-1, keepdims=True)
    el2 = jnp.where(lane == i1, neg_inf, el)
    e2 = jnp.max(el2, axis=-1, keepdims=True)
    i2 = jnp.min(jnp.where(el2 == e2, lane, LANES), axis=-1, keepdims=True)
    r = jnp.exp(e2 - e1)
    w_first = g_top / (1.0 + r)
    w_second = w_first * r

    onehot1 = jnp.where(lane == i1, 1.0, 0.0)
    onehot2 = jnp.where(lane == i2, 1.0, 0.0)
    t_row = lax.broadcasted_iota(jnp.int32, (tm, tm), 0)
    t_col = lax.broadcasted_iota(jnp.int32, (tm, tm), 1)
    earlier = jnp.where(t_col < t_row, 1.0, 0.0).astype(BF16)
    both = jnp.concatenate([onehot1, onehot2], axis=-1).astype(BF16)
    before = jnp.dot(earlier, both, preferred_element_type=F32)
    before1, before2 = before[:, :LANES], before[:, LANES:]
    cnt1 = jnp.sum(onehot1, axis=0, keepdims=True)
    cnt = cnt1 + jnp.sum(onehot2, axis=0, keepdims=True)
    run_len = ((cnt.astype(jnp.int32) + (RUN_ALIGN - 1)) // RUN_ALIGN * RUN_ALIGN).astype(F32)
    e_row = lax.broadcasted_iota(jnp.int32, (LANES, LANES), 0)
    e_col = lax.broadcasted_iota(jnp.int32, (LANES, LANES), 1)
    lower_experts = jnp.where(e_row < e_col, 1.0, 0.0)
    run_start = jnp.dot(jnp.broadcast_to(run_len, (8, LANES)), lower_experts,
                        precision=lax.Precision.HIGHEST, preferred_element_type=F32)[0:1, :]
    pos1 = jnp.sum(onehot1 * (run_start + before1), axis=-1, keepdims=True)
    pos2 = jnp.sum(onehot2 * (run_start + cnt1 + before2), axis=-1, keepdims=True)
    meta = jnp.where(lane == 0, pos1, jnp.where(lane == 1, pos2,
                     jnp.where(lane == 2, w_first, jnp.where(lane == 3, w_second, 0.0))))
    meta_ref[...] = meta
    cnt_ref[...] = jnp.broadcast_to(cnt, (8, LANES))

    eye8 = jnp.where(lax.broadcasted_iota(jnp.int32, (8, LANES), 0)
                     == lax.broadcasted_iota(jnp.int32, (8, LANES), 1), 1.0, 0.0)
    pos_rows = lax.dot_general(eye8, meta, (((1,), (1,)), ((), ())),
                               precision=lax.Precision.HIGHEST, preferred_element_type=F32)
    r_id = lax.broadcasted_iota(jnp.int32, (SORT_ROWS, tm), 0).astype(F32)
    hit = (r_id == pos_rows[0:1, :]) | (r_id == pos_rows[1:2, :])
    perm = jnp.where(hit, 1.0, 0.0).astype(BF16)
    hs_ref[...] = jnp.dot(perm, h2_hi, preferred_element_type=F32).astype(BF16)


def _merge(x, attn, pin, gates, mod, w_up, w_pool, pool_scale, w_out, gain2, w_router, b_router, layer, seq):
    t, d = x.shape
    tm = TM_MERGE
    n_tiles = t // tm
    tiles_per_seq = seq // tm
    halo_per_tile = tm // POOL_HALO
    kern = functools.partial(_merge_kernel, seq=seq)
    const2 = lambda i: (0, 0)
    return pl.pallas_call(
        kern,
        grid=(t // tm,),
        in_specs=[
            pl.BlockSpec((tm, d), lambda i: (i, 0)),
            pl.BlockSpec((tm, ATTN_WIDTH), lambda i: (i, 0)),
            pl.BlockSpec((tm, POOL_WIDTH), lambda i: (i, 0)),
            pl.BlockSpec((POOL_HALO, POOL_WIDTH), lambda i: (jnp.maximum(i * halo_per_tile - 1, 0), 0)),
            pl.BlockSpec((tm, 2 * d), lambda i: (i, 0)),
            pl.BlockSpec((None, 1, N_ADA * d), lambda i: (i // tiles_per_seq, 0, 0)),
            _resident((ATTN_WIDTH, d), layer),
            _resident((N_POOL_GROUPS, POOL_GROUP_IN, POOL_GROUP_OUT), layer),
            pl.BlockSpec((1, d), const2),
            _resident((d, d), layer),
            pl.BlockSpec((1, d), const2),
            pl.BlockSpec((d, LANES), const2),
            pl.BlockSpec((1, LANES), const2),
        ],
        out_specs=[
            pl.BlockSpec((tm, d), lambda i: (i, 0)),
            pl.BlockSpec((SORT_ROWS, d), lambda i: (i, 0)),
            pl.BlockSpec((tm, LANES), lambda i: (i, 0)),
            pl.BlockSpec((None, 8, LANES), lambda i: (i, 0, 0)),
        ],
        out_shape=[
            jax.ShapeDtypeStruct((t, d), F32),
            jax.ShapeDtypeStruct((n_tiles * SORT_ROWS, d), BF16),
            jax.ShapeDtypeStruct((t, LANES), F32),
            jax.ShapeDtypeStruct((n_tiles, 8, LANES), F32),
        ],
        scratch_shapes=[
            pltpu.VMEM((tm + 2 * POOL_HALO, POOL_WIDTH), F32),
            pltpu.VMEM((ATTN_WIDTH, d), BF16),
            pltpu.VMEM((N_POOL_GROUPS, POOL_GROUP_IN, POOL_GROUP_OUT), BF16),
            pltpu.VMEM((d, d), BF16),
        ],
        compiler_params=_params("arbitrary"),
        name="merge_router",
    )(x, attn, pin, pin, gates, mod, w_up, w_pool, pool_scale, w_out, gain2, w_router, b_router)


def _dispatch_plan(cnt, n_tiles_max):
    n_merge = cnt.shape[0]
    i32 = jnp.int32

    def before(n):
        return (jnp.arange(n, dtype=i32)[:, None] < jnp.arange(n, dtype=i32)[None, :]).astype(i32)

    run_len = (cnt + (RUN_ALIGN - 1)) // RUN_ALIGN * RUN_ALIGN
    run_start = jnp.sum(run_len[:, :, None] * before(N_EXPERTS)[None], axis=1)
    run_blocks = (run_len // RUN_ALIGN).T
    run_first = jnp.sum(run_blocks[:, :, None] * before(n_merge)[None], axis=1)
    blocks_e = jnp.sum(run_blocks, axis=1)
    tiles_e = (blocks_e + BLK_PER_TILE - 1) // BLK_PER_TILE
    tiles_first = jnp.sum(tiles_e[:, None] * before(N_EXPERTS), axis=0)
    n_used = jnp.sum(tiles_e)
    tile_id = jnp.arange(n_tiles_max, dtype=i32)
    tile_e = jnp.sum(((tiles_first + tiles_e)[None, :] <= tile_id[:, None]).astype(i32), axis=1)
    tile_e = jnp.minimum(tile_e, N_EXPERTS - 1)
    of_tile = (tile_e[:, None] == jnp.arange(N_EXPERTS, dtype=i32)[None, :]).astype(i32)

    def per_tile(table):
        return jnp.sum(of_tile[:, :, None] * table[None], axis=1)

    tile_first = jnp.sum(of_tile * tiles_first[None, :], axis=1)
    tile_blocks = jnp.sum(of_tile * blocks_e[None, :], axis=1)
    first_t, nblk_t, start_t = per_tile(run_first), per_tile(run_blocks), per_tile(run_start.T)

    slot = jnp.arange(BLK_PER_TILE, dtype=i32)[None, :]
    blk = (tile_id - tile_first)[:, None] * BLK_PER_TILE + slot
    valid = (tile_id[:, None] < n_used) & (blk < tile_blocks[:, None])
    off = blk[:, :, None] - first_t[:, None, :]
    in_run = (off >= 0) & (off < nblk_t[:, None, :])
    row = jnp.arange(n_merge, dtype=i32) * SORT_ROWS + start_t[:, None, :] + off * RUN_ALIGN
    src = jnp.sum(jnp.where(in_run, row, 0), axis=-1)
    assert n_merge * SPARE_BLOCKS >= 2 * BLK_PER_TILE
    k = (tile_id[:, None] % 2) * BLK_PER_TILE + slot
    trash = (k // SPARE_BLOCKS) * SORT_ROWS + SORT_USED + (k % SPARE_BLOCKS) * RUN_ALIGN
    dst = jnp.where(valid, src, trash).astype(jnp.int32).reshape(-1)
    src = jnp.where(valid, src, jnp.where(valid[:, 0:1], src[:, 0:1], 0)).astype(jnp.int32).reshape(-1)
    return tile_e, n_used.astype(jnp.int32).reshape(1), src, dst


def _expert_kernel(tile_e_ref, n_used_ref, src_ref, dst_ref,
                   hs_ref, w1_ref, w3_ref, w2_ref, ys_ref,
                   xbuf, ybuf, w1b, w3b, w2b, gsem, ssem):
    i = pl.program_id(0)
    n_used = n_used_ref[0]
    slot = i % 2

    def gather_copy(tile, s, m):
        src = pl.multiple_of(src_ref[tile * BLK_PER_TILE + m], RUN_ALIGN)
        return pltpu.make_async_copy(hs_ref.at[pl.ds(src, RUN_ALIGN), :],
                                     xbuf.at[s, pl.ds(m * RUN_ALIGN, RUN_ALIGN), :], gsem.at[s])

    def scatter_copy(tile, s, m):
        dst = pl.multiple_of(dst_ref[tile * BLK_PER_TILE + m], RUN_ALIGN)
        return pltpu.make_async_copy(ybuf.at[s, pl.ds(m * RUN_ALIGN, RUN_ALIGN), :],
                                     ys_ref.at[pl.ds(dst, RUN_ALIGN), :], ssem.at[s])

    def start_all(copy, tile, s):
        for m in range(BLK_PER_TILE):
            copy(tile, s, m).start()

    def wait_all(copy, tile, s):
        del tile
        if copy is gather_copy:
            pltpu.make_async_copy(hs_ref.at[pl.ds(0, TR_MOE), :], xbuf.at[s], gsem.at[s]).wait()
        else:
            pltpu.make_async_copy(ybuf.at[s], ys_ref.at[pl.ds(0, TR_MOE), :], ssem.at[s]).wait()

    @pl.when(i == 0)
    def _():
        start_all(gather_copy, 0, 0)

    @pl.when(i < n_used)
    def _():
        e = tile_e_ref[i]
        e_prev = tile_e_ref[jnp.maximum(i - 1, 0)]

        @pl.when((i == 0) | (e != e_prev))
        def _():
            w1b[...] = w1_ref[...].astype(BF16)
            w3b[...] = w3_ref[...].astype(BF16)
            w2b[...] = w2_ref[...].astype(BF16)

        wait_all(gather_copy, i, slot)

        @pl.when(i >= 2)
        def _():
            wait_all(scatter_copy, i - 2, slot)

        nxt = jnp.minimum(i + 1, n_used - 1)
        start_all(gather_copy, nxt, 1 - slot)
        x = xbuf[slot]
        a = jnp.dot(x, w1b[...], preferred_element_type=F32)
        b = jnp.dot(x, w3b[...], preferred_element_type=F32)
        hid = (a * jax.nn.sigmoid(a) * b).astype(BF16)
        ybuf[slot] = jnp.dot(hid, w2b[...], preferred_element_type=F32).astype(BF16)

        @pl.when(i == n_used - 1)
        def _():
            wait_all(gather_copy, nxt, 1 - slot)

        start_all(scatter_copy, i, slot)

        @pl.when(i == n_used - 1)
        def _():
            @pl.when(i >= 1)
            def _():
                wait_all(scatter_copy, i - 1, 1 - slot)
            wait_all(scatter_copy, i, slot)


def _experts(hs, cnt, w1, w3, w2, layer):
    n_rows, d = hs.shape
    f = w1.shape[-1]
    first_expert = layer * N_EXPERTS
    n_tiles_max = n_rows // TR_MOE + N_EXPERTS
    tile_e, n_used, src, dst = _dispatch_plan(cnt, n_tiles_max)
    w_in_map = lambda i, te, nu, s, dd: (first_expert + te[i], 0, 0)
    grid_spec = pltpu.PrefetchScalarGridSpec(
        num_scalar_prefetch=4,
        grid=(n_tiles_max,),
        in_specs=[
            pl.BlockSpec(memory_space=pl.ANY),
            pl.BlockSpec((None, d, f), w_in_map),
            pl.BlockSpec((None, d, f), w_in_map),
            pl.BlockSpec((None, f, d), w_in_map),
        ],
        out_specs=pl.BlockSpec(memory_space=pl.ANY),
        scratch_shapes=[
            pltpu.VMEM((2, TR_MOE, d), BF16),
            pltpu.VMEM((2, TR_MOE, d), BF16),
            pltpu.VMEM((d, f), BF16),
            pltpu.VMEM((d, f), BF16),
            pltpu.VMEM((f, d), BF16),
            pltpu.SemaphoreType.DMA((2,)),
            pltpu.SemaphoreType.DMA((2,)),
        ],
    )
    return pl.pallas_call(
        _expert_kernel,
        grid_spec=grid_spec,
        out_shape=jax.ShapeDtypeStruct(hs.shape, hs.dtype),
        input_output_aliases={4: 0},
        compiler_params=_params("arbitrary"),
        name="experts",
    )(tile_e, n_used, src, dst, hs, w1, w3, w2)


def _combine_kernel(ys_ref, meta_ref, x1_ref, mod_ref, o_ref):
    tm = x1_ref.shape[0]
    d = D_MODEL
    meta = meta_ref[...]
    r_id = lax.broadcasted_iota(jnp.int32, (tm, SORT_ROWS), 1).astype(F32)
    ys = ys_ref[...]
    pick1 = jnp.where(r_id == meta[:, 0:1], 1.0, 0.0).astype(BF16)
    pick2 = jnp.where(r_id == meta[:, 1:2], 1.0, 0.0).astype(BF16)
    y = (meta[:, 2:3] * jnp.dot(pick1, ys, preferred_element_type=F32)
         + meta[:, 3:4] * jnp.dot(pick2, ys, preferred_element_type=F32))
    o_ref[...] = x1_ref[...] + mod_ref[:, 5 * d:6 * d] * y


def _combine(ys, meta, x1, mod, seq):
    t, d = x1.shape
    tm = TM_MERGE
    tiles_per_seq = seq // tm
    return pl.pallas_call(
        _combine_kernel,
        grid=(t // tm,),
        in_specs=[
            pl.BlockSpec((SORT_ROWS, d), lambda i: (i, 0)),
            pl.BlockSpec((tm, LANES), lambda i: (i, 0)),
            pl.BlockSpec((tm, d), lambda i: (i, 0)),
            pl.BlockSpec((None, 1, N_ADA * d), lambda i: (i // tiles_per_seq, 0, 0)),
        ],
        out_specs=pl.BlockSpec((tm, d), lambda i: (i, 0)),
        out_shape=jax.ShapeDtypeStruct((t, d), F32),
        compiler_params=_params("parallel"),
        name="combine",
    )(ys, meta, x1, mod)


@jax.jit
def kernel(x, c, w_ada, b_ada, norm1, w_in, q_norm, k_norm, w_attn_up, w_pool, pool_scale, w_out,
           norm2, w_router_g, b_router_g, w_router_e, b_router_e, w1, w3, w2):
    bsz, seq, d = x.shape
    n_layers = w_ada.shape[0]
    t = bsz * seq
    mod_all = _ada(c, w_ada, b_ada).reshape(n_layers, bsz, 1, N_ADA * d)

    pad = LANES - N_EXPERTS - N_EXPERT_GROUPS
    w_router = jnp.concatenate(
        [w_router_e, w_router_g, jnp.zeros((n_layers, d, pad), F32)], axis=-1)
    b_router = jnp.concatenate(
        [b_router_e, b_router_g, jnp.zeros((n_layers, pad), F32)], axis=-1).reshape(n_layers, 1, LANES)
    heads_per_block = LANES // HEAD_DIM
    q_gain = jnp.tile(q_norm, (1, heads_per_block)).reshape(n_layers, 1, LANES)
    k_gain = jnp.tile(k_norm, (1, heads_per_block)).reshape(n_layers, 1, LANES)

    w1_all = w1.reshape(n_layers * N_EXPERTS, d, EXPERT_HIDDEN)
    w3_all = w3.reshape(n_layers * N_EXPERTS, d, EXPERT_HIDDEN)
    w2_all = w2.reshape(n_layers * N_EXPERTS, EXPERT_HIDDEN, d)

    xt = x.reshape(t, d)
    for l in range(n_layers):
        mod = mod_all[l]
        qkv, pin, gates = _in_proj(xt, mod, norm1[l].reshape(1, d), w_in, l, seq)
        attn = _attention(qkv, q_gain[l], k_gain[l], seq)
        x1, hs, meta, cnt = _merge(xt, attn, pin, gates, mod, w_attn_up, w_pool, pool_scale[l].reshape(1, d),
                                   w_out, norm2[l].reshape(1, d), w_router[l], b_router[l], l, seq)
        cnt = cnt[:, 0, :N_EXPERTS].astype(jnp.int32)
        ys = _experts(hs, cnt, w1_all, w3_all, w2_all, l)
        xt = _combine(ys, meta, x1, mod, seq)
    return xt.reshape(bsz, seq, d)
```

```python
import functools

import jax
import jax.numpy as jnp
from jax import lax
from jax.experimental import pallas as pl
from jax.experimental.pallas import tpu as pltpu

F32 = jnp.float32
BF16 = jnp.bfloat16

D_MODEL = 1024
N_HEADS = 8
HEAD_DIM = 64
ATTN_WIDTH = N_HEADS * HEAD_DIM
MOBA_BLOCK = 256
MOBA_TOPK = 3
POOL_WINDOWS = (2, 4, 8, 16)
N_POOL_GROUPS = len(POOL_WINDOWS)
POOL_WIDTH = D_MODEL // 2
POOL_GROUP_IN = POOL_WIDTH // N_POOL_GROUPS
POOL_GROUP_OUT = D_MODEL // N_POOL_GROUPS
N_EXPERT_GROUPS = 4
EXPERTS_PER_GROUP = 8
N_EXPERTS = N_EXPERT_GROUPS * EXPERTS_PER_GROUP
EXPERT_HIDDEN = 256
N_ADA = 6
EPS = 1e-6
LOG2_E = 1.4426950408889634
MASK_BIAS = -1e30

LANES = 128
POOL_HALO = 16
QKV_WIDTH = 3 * ATTN_WIDTH
GATE_OFF = QKV_WIDTH + POOL_WIDTH
VMEM_LIMIT = 56 * 1024 * 1024

TM_IN = 512
TM_MERGE = 512
TN_ADA = 1536
RUN_ALIGN = 16
SORT_USED = 2 * TM_MERGE + N_EXPERTS * (RUN_ALIGN - 1)
SPARE_BLOCKS = 2
SORT_ROWS = SORT_USED + SPARE_BLOCKS * RUN_ALIGN
TR_MOE = 512
BLK_PER_TILE = TR_MOE // RUN_ALIGN
GATHER_DEPTH = 3


def _params(*sem):
    return pltpu.CompilerParams(dimension_semantics=sem, vmem_limit_bytes=VMEM_LIMIT)


def _ada_kernel(c_ref, w_ref, b_ref, o_ref):
    c = c_ref[...]
    c_act = (c * jax.nn.sigmoid(c)).astype(BF16)
    o_ref[...] = jnp.dot(c_act, w_ref[...].astype(BF16), preferred_element_type=F32) + b_ref[...]


def _ada(c, w_ada, b_ada):
    n_layers, d, n = w_ada.shape
    b = c.shape[0]
    return pl.pallas_call(
        _ada_kernel,
        grid=(n_layers, n // TN_ADA),
        in_specs=[
            pl.BlockSpec((b, d), lambda l, j: (0, 0)),
            pl.BlockSpec((None, d, TN_ADA), lambda l, j: (l, 0, j)),
            pl.BlockSpec((None, 1, TN_ADA), lambda l, j: (l, 0, j)),
        ],
        out_specs=pl.BlockSpec((None, b, TN_ADA), lambda l, j: (l, 0, j)),
        out_shape=jax.ShapeDtypeStruct((n_layers, b, n), F32),
        compiler_params=_params("parallel", "parallel"),
        name="ada",
    )(c, w_ada, b_ada.reshape(n_layers, 1, n))


def _modulated_norm(x, gain, shift, scale):
    ms = jnp.mean(x * x, axis=-1, keepdims=True)
    xn = x * lax.rsqrt(ms + EPS) * gain
    return xn * (1.0 + scale) + shift


def _cast_once(src_ref, dst_ref, chunk):
    @pl.when(pl.program_id(0) == 0)
    def _():
        for c in range(0, src_ref.shape[-1], chunk):
            dst_ref[..., c:c + chunk] = src_ref[..., c:c + chunk].astype(BF16)


def _in_kernel(x_ref, mod_ref, g_ref, w32_ref, qkv_ref, pin_ref, gates_ref, w_ref):
    _cast_once(w32_ref, w_ref, POOL_WIDTH)
    h = _modulated_norm(x_ref[...], g_ref[...], mod_ref[:, 0:D_MODEL],
                        mod_ref[:, D_MODEL:2 * D_MODEL]).astype(BF16)
    qkv_ref[...] = jnp.dot(h, w_ref[:, 0:QKV_WIDTH], preferred_element_type=F32).astype(BF16)
    pin_ref[...] = jnp.dot(h, w_ref[:, QKV_WIDTH:GATE_OFF], preferred_element_type=F32).astype(BF16)
    gates_ref[...] = jnp.dot(h, w_ref[:, GATE_OFF:], preferred_element_type=F32).astype(BF16)


def _resident(shape, layer):
    zeros = (0,) * len(shape)
    return pl.BlockSpec((None,) + shape, lambda i: (layer,) + zeros, pipeline_mode=pl.Buffered(1))


def _in_proj(x, mod, gain, w_in, layer, seq):
    t, d = x.shape
    n = w_in.shape[-1]
    tiles_per_seq = seq // TM_IN
    return pl.pallas_call(
        _in_kernel,
        grid=(t // TM_IN,),
        in_specs=[
            pl.BlockSpec((TM_IN, d), lambda i: (i, 0)),
            pl.BlockSpec((None, 1, N_ADA * d), lambda i: (i // tiles_per_seq, 0, 0)),
            pl.BlockSpec((1, d), lambda i: (0, 0)),
            _resident((d, n), layer),
        ],
        out_specs=[
            pl.BlockSpec((TM_IN, QKV_WIDTH), lambda i: (i, 0)),
            pl.BlockSpec((TM_IN, POOL_WIDTH), lambda i: (i, 0)),
            pl.BlockSpec((TM_IN, 2 * d), lambda i: (i, 0)),
        ],
        out_shape=[
            jax.ShapeDtypeStruct((t, QKV_WIDTH), BF16),
            jax.ShapeDtypeStruct((t, POOL_WIDTH), BF16),
            jax.ShapeDtypeStruct((t, 2 * d), BF16),
        ],
        scratch_shapes=[pltpu.VMEM((d, n), BF16)],
        compiler_params=_params("arbitrary"),
        name="in_proj",
    )(x, mod, gain, w_in)


def _att_kernel(q_ref, k_ref, v_ref, qg_ref, kg_ref, o_ref,
                ka_ref, kb_ref, qa_ref, qb_ref, va_ref, vb_ref, km_ref, *, seq):
    nb = seq // MOBA_BLOCK
    lane = lax.broadcasted_iota(jnp.int32, (1, LANES), 1)
    is_a = lane < HEAD_DIM
    m_a = is_a.astype(F32)
    m_b = 1.0 - m_a
    nt = (((1,), (1,)), ((), ()))
    neg_inf = -jnp.inf

    def head_norm(z, gain):
        z2 = z * z
        ss_a = jnp.sum(z2 * m_a, axis=-1, keepdims=True)
        ss_b = jnp.sum(z2 * m_b, axis=-1, keepdims=True)
        r = jnp.where(is_a, lax.rsqrt(ss_a * (1.0 / HEAD_DIM) + EPS), lax.rsqrt(ss_b * (1.0 / HEAD_DIM) + EPS))
        return z * (r * gain)

    for j in range(nb):
        rows = pl.ds(j * MOBA_BLOCK, MOBA_BLOCK)
        kn = head_norm(k_ref[rows, :].astype(F32), kg_ref[...])
        km_ref[j:j + 1, :] = jnp.mean(kn, axis=0, keepdims=True)
        ka_ref[rows, :] = jnp.where(is_a, kn, (lane == HEAD_DIM + j).astype(F32)).astype(BF16)
        kb_ref[rows, :] = jnp.where(is_a, (lane == j).astype(F32), kn).astype(BF16)
        v = v_ref[rows, :]
        va_ref[rows, :] = jnp.where(is_a, v, (lane == HEAD_DIM).astype(BF16))
        vb_ref[rows, :] = jnp.where(is_a, (lane == 0).astype(BF16), v)

    km = km_ref[...]
    km_both = jnp.concatenate([km * m_a, km * m_b], axis=0)
    blk = lax.broadcasted_iota(jnp.int32, (nb, 1), 0)
    row_id = lax.broadcasted_iota(jnp.int32, (MOBA_BLOCK, MOBA_BLOCK), 0)
    col_id = lax.broadcasted_iota(jnp.int32, (MOBA_BLOCK, MOBA_BLOCK), 1)
    causal = col_id <= row_id
    eye = jnp.where(col_id == row_id, 1.0, 0.0).astype(BF16)

    qn_all = head_norm(q_ref[...].astype(F32), qg_ref[...])
    gates = lax.dot_general(km_both, qn_all, nt, precision=lax.Precision.HIGHEST,
                            preferred_element_type=F32)
    own = lax.broadcasted_iota(jnp.int32, (1, seq), 1) // MOBA_BLOCK
    bias_t = []
    for h in range(2):
        gate = jnp.where(blk < own, gates[h * nb:(h + 1) * nb, :], neg_inf)
        rank = jnp.zeros(gate.shape, jnp.int32)
        for l in range(nb - 1):
            g_l = gate[l:l + 1, :]
            ahead = (g_l > gate) | ((g_l == gate) & (l < blk))
            rank = rank + ahead.astype(jnp.int32)
        attended = ((rank < MOBA_TOPK) & (blk < own)) | (blk == own)
        bias_t.append(jnp.where(attended, 0.0, MASK_BIAS))
    pad = jnp.zeros((HEAD_DIM - nb, seq), F32)
    bias_t = jnp.concatenate([bias_t[1], pad, bias_t[0], pad], axis=0).astype(BF16)
    for i in range(nb):
        qrows = pl.ds(i * MOBA_BLOCK, MOBA_BLOCK)
        bias = lax.dot_general(eye, bias_t[:, i * MOBA_BLOCK:(i + 1) * MOBA_BLOCK], nt,
                               preferred_element_type=F32)
        q_aug = qn_all[i * MOBA_BLOCK:(i + 1) * MOBA_BLOCK] * (HEAD_DIM ** -0.5 * LOG2_E)
        qa_ref[qrows, :] = jnp.where(is_a, q_aug, bias).astype(BF16)
        qb_ref[qrows, :] = jnp.where(is_a, bias, q_aug).astype(BF16)

    for i in reversed(range(nb)):
        qrows = pl.ds(i * MOBA_BLOCK, MOBA_BLOCK)
        outs = []
        for q_ref_h, k_ref_h, v_ref_h, sum_lane in ((qa_ref, ka_ref, va_ref, HEAD_DIM),
                                                   (qb_ref, kb_ref, vb_ref, 0)):
            qh = q_ref_h[qrows, :]
            s_blocks = []
            m_run = None
            for j in range(i + 1):
                krows = pl.ds(j * MOBA_BLOCK, MOBA_BLOCK)
                s = lax.dot_general(qh, k_ref_h[krows, :], nt, preferred_element_type=F32)
                if j == i:
                    s = jnp.where(causal, s, neg_inf)
                s_blocks.append(s)
                m_blk = jnp.maximum(s[:, :LANES], s[:, LANES:])
                m_run = m_blk if m_run is None else jnp.maximum(m_run, m_blk)
            m_row = jnp.max(m_run, axis=-1, keepdims=True)
            p_blocks = [jnp.exp2(s - m_row).astype(BF16) for s in s_blocks]
            p_all = p_blocks[0] if i == 0 else jnp.concatenate(p_blocks, axis=-1)
            acc = jnp.dot(p_all, v_ref_h[0:(i + 1) * MOBA_BLOCK, :], preferred_element_type=F32)
            outs.append(acc * (1.0 / acc[:, sum_lane:sum_lane + 1]))
        o_ref[qrows, :] = jnp.where(is_a, outs[0], outs[1]).astype(o_ref.dtype)


def _attention(qkv, q_gain, k_gain, seq):
    t = qkv.shape[0]
    n_pairs = ATTN_WIDTH // LANES
    kern = functools.partial(_att_kernel, seq=seq)
    return pl.pallas_call(
        kern,
        grid=(t // seq, n_pairs),
        in_specs=[
            pl.BlockSpec((seq, LANES), lambda b, h: (b, h)),
            pl.BlockSpec((seq, LANES), lambda b, h: (b, n_pairs + h)),
            pl.BlockSpec((seq, LANES), lambda b, h: (b, 2 * n_pairs + h)),
            pl.BlockSpec((1, LANES), lambda b, h: (0, 0)),
            pl.BlockSpec((1, LANES), lambda b, h: (0, 0)),
        ],
        out_specs=pl.BlockSpec((seq, LANES), lambda b, h: (b, h)),
        out_shape=jax.ShapeDtypeStruct((t, ATTN_WIDTH), BF16),
        scratch_shapes=[
            *[pltpu.VMEM((seq, LANES), BF16)] * 6,
            pltpu.VMEM((seq // MOBA_BLOCK, LANES), F32),
        ],
        compiler_params=_params("parallel", "parallel"),
        name="moba_attention",
    )(qkv, qkv, qkv, q_gain, k_gain)


def _merge_kernel(x_ref, attn_ref, pin_ref, halo_ref, gates_ref, mod_ref, wup32_ref, wpool32_ref,
                  pscale_ref, wout32_ref, g2_ref, wr_ref, br_ref,
                  x1_ref, hs_ref, meta_ref, cnt_ref, pool_ref, wup_ref, wpool_ref, wout_ref, *, seq):
    tm = x_ref.shape[0]
    d = D_MODEL
    _cast_once(wup32_ref, wup_ref, d)
    _cast_once(wpool32_ref, wpool_ref, POOL_GROUP_OUT)
    _cast_once(wout32_ref, wout_ref, d)
    seq_pos = (pl.program_id(0) * tm) % seq

    top = 2 * POOL_HALO
    halo = halo_ref[...].astype(F32)
    pool_ref[0:POOL_HALO, :] = jnp.zeros((POOL_HALO, POOL_WIDTH), F32)
    pool_ref[POOL_HALO:top, :] = jnp.where(seq_pos == 0, 0.0, halo)
    pool_ref[top:, :] = pin_ref[...].astype(F32)
    pos = seq_pos + lax.broadcasted_iota(jnp.int32, (tm, 1), 0)

    a_up = jnp.dot(attn_ref[...], wup_ref[...], preferred_element_type=F32)
    merged = []
    for g, w in enumerate(POOL_WINDOWS):
        cols = slice(g * POOL_GROUP_IN, (g + 1) * POOL_GROUP_IN)
        tok = pool_ref[top:, cols]
        shift = 1
        while shift < w:
            pool_ref[POOL_HALO:, cols] = (pool_ref[POOL_HALO:, cols]
                                         + pool_ref[POOL_HALO - shift:top + tm - shift, cols])
            shift *= 2
        win = pool_ref[top:, cols]
        cnt = jnp.minimum(pos + 1, w).astype(F32)
        pooled = (win / cnt - tok).astype(BF16)
        ocols = slice(g * POOL_GROUP_OUT, (g + 1) * POOL_GROUP_OUT)
        b_pool = jnp.dot(pooled, wpool_ref[g], preferred_element_type=F32) * pscale_ref[:, ocols]
        ga = gates_ref[:, ocols].astype(F32)
        gp = gates_ref[:, d + g * POOL_GROUP_OUT:d + (g + 1) * POOL_GROUP_OUT].astype(F32)
        merged.append((jax.nn.sigmoid(ga) * a_up[:, ocols] + jax.nn.sigmoid(gp) * b_pool).astype(BF16))
    merged = jnp.concatenate(merged, axis=-1)
    y = jnp.dot(merged, wout_ref[...], preferred_element_type=F32)
    x1 = x_ref[...] + mod_ref[:, 2 * d:3 * d] * y
    x1_ref[...] = x1

    h2 = _modulated_norm(x1, g2_ref[...], mod_ref[:, 3 * d:4 * d], mod_ref[:, 4 * d:5 * d])

    h2_hi = h2.astype(BF16)
    h2_lo = (h2 - h2_hi.astype(F32)).astype(BF16)
    wr = wr_ref[...]
    wr_hi = wr.astype(BF16)
    wr_lo = (wr - wr_hi.astype(F32)).astype(BF16)
    logits = (jnp.dot(h2_hi, wr_hi, preferred_element_type=F32)
              + jnp.dot(h2_lo, wr_hi, preferred_element_type=F32)
              + jnp.dot(h2_hi, wr_lo, preferred_element_type=F32)) + br_ref[...]
    lane = lax.broadcasted_iota(jnp.int32, (1, LANES), 1)
    neg_inf = -jnp.inf
    is_group = (lane >= N_EXPERTS) & (lane < N_EXPERTS + N_EXPERT_GROUPS)
    gl = jnp.where(is_group, logits, neg_inf)
    g_max = jnp.max(gl, axis=-1, keepdims=True)
    g_top = 1.0 / jnp.sum(jnp.exp(gl - g_max), axis=-1, keepdims=True)
    g_idx = jnp.min(jnp.where(gl == g_max, lane, LANES), axis=-1, keepdims=True) - N_EXPERTS
    in_group = (lane < N_EXPERTS) & ((lane // EXPERTS_PER_GROUP) == g_idx)
    el = jnp.where(in_group, logits, neg_inf)
    e1 = jnp.max(el, axis=-1, keepdims=True)
    i1 = jnp.min(jnp.where(el == e1, lane, LANES), axis=-1, keepdims=True)
    el2 = jnp.where(lane == i1, neg_inf, el)
    e2 = jnp.max(el2, axis=-1, keepdims=True)
    i2 = jnp.min(jnp.where(el2 == e2, lane, LANES), axis=-1, keepdims=True)
    r = jnp.exp(e2 - e1)
    w_first = g_top / (1.0 + r)
    w_second = w_first * r

    onehot1 = jnp.where(lane == i1, 1.0, 0.0)
    onehot2 = jnp.where(lane == i2, 1.0, 0.0)
    t_row = lax.broadcasted_iota(jnp.int32, (tm, tm), 0)
    t_col = lax.broadcasted_iota(jnp.int32, (tm, tm), 1)
    earlier = jnp.where(t_col < t_row, 1.0, 0.0).astype(BF16)
    both = jnp.concatenate([onehot1, onehot2], axis=-1).astype(BF16)
    before = jnp.dot(earlier, both, preferred_element_type=F32)
    before1, before2 = before[:, :LANES], before[:, LANES:]
    cnt1 = jnp.sum(onehot1, axis=0, keepdims=True)
    cnt = cnt1 + jnp.sum(onehot2, axis=0, keepdims=True)
    run_len = ((cnt.astype(jnp.int32) + (RUN_ALIGN - 1)) // RUN_ALIGN * RUN_ALIGN).astype(F32)
    e_row = lax.broadcasted_iota(jnp.int32, (LANES, LANES), 0)
    e_col = lax.broadcasted_iota(jnp.int32, (LANES, LANES), 1)
    lower_experts = jnp.where(e_row < e_col, 1.0, 0.0)
    run_start = jnp.dot(jnp.broadcast_to(run_len, (8, LANES)), lower_experts,
                        precision=lax.Precision.HIGHEST, preferred_element_type=F32)[0:1, :]
    pos1 = jnp.sum(onehot1 * (run_start + before1), axis=-1, keepdims=True)
    pos2 = jnp.sum(onehot2 * (run_start + cnt1 + before2), axis=-1, keepdims=True)
    meta = jnp.where(lane == 0, pos1, jnp.where(lane == 1, pos2,
                     jnp.where(lane == 2, w_first, jnp.where(lane == 3, w_second, 0.0))))
    meta_ref[...] = meta
    cnt_ref[...] = jnp.broadcast_to(cnt, (8, LANES))

    eye8 = jnp.where(lax.broadcasted_iota(jnp.int32, (8, LANES), 0)
                     == lax.broadcasted_iota(jnp.int32, (8, LANES), 1), 1.0, 0.0)
    pos_rows = lax.dot_general(eye8, meta, (((1,), (1,)), ((), ())),
                               precision=lax.Precision.HIGHEST, preferred_element_type=F32)
    r_id = lax.broadcasted_iota(jnp.int32, (SORT_ROWS, tm), 0).astype(F32)
    hit = (r_id == pos_rows[0:1, :]) | (r_id == pos_rows[1:2, :])
    perm = jnp.where(hit, 1.0, 0.0).astype(BF16)
    hs_ref[...] = jnp.dot(perm, h2_hi, preferred_element_type=F32).astype(BF16)


def _merge(x, attn, pin, gates, mod, w_up, w_pool, pool_scale, w_out, gain2, w_router, b_router, layer, seq):
    t, d = x.shape
    tm = TM_MERGE
    n_tiles = t // tm
    tiles_per_seq = seq // tm
    halo_per_tile = tm // POOL_HALO
    kern = functools.partial(_merge_kernel, seq=seq)
    const2 = lambda i: (0, 0)
    return pl.pallas_call(
        kern,
        grid=(t // tm,),
        in_specs=[
            pl.BlockSpec((tm, d), lambda i: (i, 0)),
            pl.BlockSpec((tm, ATTN_WIDTH), lambda i: (i, 0)),
            pl.BlockSpec((tm, POOL_WIDTH), lambda i: (i, 0)),
            pl.BlockSpec((POOL_HALO, POOL_WIDTH), lambda i: (jnp.maximum(i * halo_per_tile - 1, 0), 0)),
            pl.BlockSpec((tm, 2 * d), lambda i: (i, 0)),
            pl.BlockSpec((None, 1, N_ADA * d), lambda i: (i // tiles_per_seq, 0, 0)),
            _resident((ATTN_WIDTH, d), layer),
            _resident((N_POOL_GROUPS, POOL_GROUP_IN, POOL_GROUP_OUT), layer),
            pl.BlockSpec((1, d), const2),
            _resident((d, d), layer),
            pl.BlockSpec((1, d), const2),
            pl.BlockSpec((d, LANES), const2),
            pl.BlockSpec((1, LANES), const2),
        ],
        out_specs=[
            pl.BlockSpec((tm, d), lambda i: (i, 0)),
            pl.BlockSpec((SORT_ROWS, d), lambda i: (i, 0)),
            pl.BlockSpec((tm, LANES), lambda i: (i, 0)),
            pl.BlockSpec((None, 8, LANES), lambda i: (i, 0, 0)),
        ],
        out_shape=[
            jax.ShapeDtypeStruct((t, d), F32),
            jax.ShapeDtypeStruct((n_tiles * SORT_ROWS, d), BF16),
            jax.ShapeDtypeStruct((t, LANES), F32),
            jax.ShapeDtypeStruct((n_tiles, 8, LANES), F32),
        ],
        scratch_shapes=[
            pltpu.VMEM((tm + 2 * POOL_HALO, POOL_WIDTH), F32),
            pltpu.VMEM((ATTN_WIDTH, d), BF16),
            pltpu.VMEM((N_POOL_GROUPS, POOL_GROUP_IN, POOL_GROUP_OUT), BF16),
            pltpu.VMEM((d, d), BF16),
        ],
        compiler_params=_params("arbitrary"),
        name="merge_router",
    )(x, attn, pin, pin, gates, mod, w_up, w_pool, pool_scale, w_out, gain2, w_router, b_router)


def _dispatch_plan(cnt, n_tiles_max):
    n_merge = cnt.shape[0]
    i32 = jnp.int32

    def before(n):
        return (jnp.arange(n, dtype=i32)[:, None] < jnp.arange(n, dtype=i32)[None, :]).astype(i32)

    run_len = (cnt + (RUN_ALIGN - 1)) // RUN_ALIGN * RUN_ALIGN
    run_start = jnp.sum(run_len[:, :, None] * before(N_EXPERTS)[None], axis=1)
    run_blocks = (run_len // RUN_ALIGN).T
    run_first = jnp.sum(run_blocks[:, :, None] * before(n_merge)[None], axis=1)
    blocks_e = jnp.sum(run_blocks, axis=1)
    tiles_e = (blocks_e + BLK_PER_TILE - 1) // BLK_PER_TILE
    tiles_first = jnp.sum(tiles_e[:, None] * before(N_EXPERTS), axis=0)
    n_used = jnp.sum(tiles_e)
    tile_id = jnp.arange(n_tiles_max, dtype=i32)
    tile_e = jnp.sum(((tiles_first + tiles_e)[None, :] <= tile_id[:, None]).astype(i32), axis=1)
    tile_e = jnp.minimum(tile_e, N_EXPERTS - 1)
    of_tile = (tile_e[:, None] == jnp.arange(N_EXPERTS, dtype=i32)[None, :]).astype(i32)

    def per_tile(table):
        return jnp.sum(of_tile[:, :, None] * table[None], axis=1)

    tile_first = jnp.sum(of_tile * tiles_first[None, :], axis=1)
    tile_blocks = jnp.sum(of_tile * blocks_e[None, :], axis=1)
    first_t, nblk_t, start_t = per_tile(run_first), per_tile(run_blocks), per_tile(run_start.T)

    slot = jnp.arange(BLK_PER_TILE, dtype=i32)[None, :]
    blk = (tile_id - tile_first)[:, None] * BLK_PER_TILE + slot
    valid = (tile_id[:, None] < n_used) & (blk < tile_blocks[:, None])
    off = blk[:, :, None] - first_t[:, None, :]
    in_run = (off >= 0) & (off < nblk_t[:, None, :])
    row = jnp.arange(n_merge, dtype=i32) * SORT_ROWS + start_t[:, None, :] + off * RUN_ALIGN
    src = jnp.sum(jnp.where(in_run, row, 0), axis=-1)
    assert n_merge * SPARE_BLOCKS >= 2 * BLK_PER_TILE
    k = (tile_id[:, None] % 2) * BLK_PER_TILE + slot
    trash = (k // SPARE_BLOCKS) * SORT_ROWS + SORT_USED + (k % SPARE_BLOCKS) * RUN_ALIGN
    dst = jnp.where(valid, src, trash).astype(jnp.int32).reshape(-1)
    src = jnp.where(valid, src, jnp.where(valid[:, 0:1], src[:, 0:1], 0)).astype(jnp.int32).reshape(-1)
    return tile_e, n_used.astype(jnp.int32).reshape(1), src, dst


def _expert_kernel(tile_e_ref, n_used_ref, src_ref, dst_ref,
                   hs_ref, w1_ref, w3_ref, w2_ref, ys_ref,
                   xbuf, ybuf, w1b, w3b, w2b, gsem, ssem):
    i = pl.program_id(0)
    n_used = n_used_ref[0]
    slot = i % 2

    def gather_copy(tile, s, m):
        src = pl.multiple_of(src_ref[tile * BLK_PER_TILE + m], RUN_ALIGN)
        return pltpu.make_async_copy(hs_ref.at[pl.ds(src, RUN_ALIGN), :],
                                     xbuf.at[s, pl.ds(m * RUN_ALIGN, RUN_ALIGN), :], gsem.at[s])

    def scatter_copy(tile, s, m):
        dst = pl.multiple_of(dst_ref[tile * BLK_PER_TILE + m], RUN_ALIGN)
        return pltpu.make_async_copy(ybuf.at[s, pl.ds(m * RUN_ALIGN, RUN_ALIGN), :],
                                     ys_ref.at[pl.ds(dst, RUN_ALIGN), :], ssem.at[s])

    def start_all(copy, tile, s):
        for m in range(BLK_PER_TILE):
            copy(tile, s, m).start()

    def wait_all(copy, tile, s):
        del tile
        if copy is gather_copy:
            pltpu.make_async_copy(hs_ref.at[pl.ds(0, TR_MOE), :], xbuf.at[s], gsem.at[s]).wait()
        else:
            pltpu.make_async_copy(ybuf.at[s], ys_ref.at[pl.ds(0, TR_MOE), :], ssem.at[s]).wait()

    xslot = i % GATHER_DEPTH

    @pl.when(i == 0)
    def _():
        for t in range(GATHER_DEPTH - 1):
            @pl.when(t < n_used)
            def _():
                start_all(gather_copy, t, t)

    @pl.when(i + (GATHER_DEPTH - 1) < n_used)
    def _():
        start_all(gather_copy, i + (GATHER_DEPTH - 1), (i + (GATHER_DEPTH - 1)) % GATHER_DEPTH)

    @pl.when(i < n_used)
    def _():
        e = tile_e_ref[i]
        e_prev = tile_e_ref[jnp.maximum(i - 1, 0)]

        @pl.when((i == 0) | (e != e_prev))
        def _():
            w1b[...] = w1_ref[...].astype(BF16)
            w3b[...] = w3_ref[...].astype(BF16)
            w2b[...] = w2_ref[...].astype(BF16)

        wait_all(gather_copy, i, xslot)

        @pl.when(i >= 2)
        def _():
            wait_all(scatter_copy, i - 2, slot)

        x = xbuf[xslot]
        a = jnp.dot(x, w1b[...], preferred_element_type=F32)
        b = jnp.dot(x, w3b[...], preferred_element_type=F32)
        hid = (a * jax.nn.sigmoid(a) * b).astype(BF16)
        ybuf[slot] = jnp.dot(hid, w2b[...], preferred_element_type=F32).astype(BF16)
        start_all(scatter_copy, i, slot)

        @pl.when(i == n_used - 1)
        def _():
            @pl.when(i >= 1)
            def _():
                wait_all(scatter_copy, i - 1, 1 - slot)
            wait_all(scatter_copy, i, slot)


def _experts(hs, cnt, w1, w3, w2, layer):
    n_rows, d = hs.shape
    f = w1.shape[-1]
    first_expert = layer * N_EXPERTS
    n_tiles_max = n_rows // TR_MOE + N_EXPERTS
    tile_e, n_used, src, dst = _dispatch_plan(cnt, n_tiles_max)
    w_in_map = lambda i, te, nu, s, dd: (first_expert + te[i], 0, 0)
    grid_spec = pltpu.PrefetchScalarGridSpec(
        num_scalar_prefetch=4,
        grid=(n_tiles_max,),
        in_specs=[
            pl.BlockSpec(memory_space=pl.ANY),
            pl.BlockSpec((None, d, f), w_in_map),
            pl.BlockSpec((None, d, f), w_in_map),
            pl.BlockSpec((None, f, d), w_in_map),
        ],
        out_specs=pl.BlockSpec(memory_space=pl.ANY),
        scratch_shapes=[
            pltpu.VMEM((GATHER_DEPTH, TR_MOE, d), BF16),
            pltpu.VMEM((2, TR_MOE, d), BF16),
            pltpu.VMEM((d, f), BF16),
            pltpu.VMEM((d, f), BF16),
            pltpu.VMEM((f, d), BF16),
            pltpu.SemaphoreType.DMA((GATHER_DEPTH,)),
            pltpu.SemaphoreType.DMA((2,)),
        ],
    )
    return pl.pallas_call(
        _expert_kernel,
        grid_spec=grid_spec,
        out_shape=jax.ShapeDtypeStruct(hs.shape, hs.dtype),
        input_output_aliases={4: 0},
        compiler_params=_params("arbitrary"),
        name="experts",
    )(tile_e, n_used, src, dst, hs, w1, w3, w2)


def _combine_kernel(ys_ref, meta_ref, x1_ref, mod_ref, o_ref):
    tm = x1_ref.shape[0]
    d = D_MODEL
    meta = meta_ref[...]
    r_id = lax.broadcasted_iota(jnp.int32, (tm, SORT_ROWS), 1).astype(F32)
    ys = ys_ref[...]
    pick1 = jnp.where(r_id == meta[:, 0:1], 1.0, 0.0).astype(BF16)
    pick2 = jnp.where(r_id == meta[:, 1:2], 1.0, 0.0).astype(BF16)
    y = (meta[:, 2:3] * jnp.dot(pick1, ys, preferred_element_type=F32)
         + meta[:, 3:4] * jnp.dot(pick2, ys, preferred_element_type=F32))
    o_ref[...] = x1_ref[...] + mod_ref[:, 5 * d:6 * d] * y


def _combine(ys, meta, x1, mod, seq):
    t, d = x1.shape
    tm = TM_MERGE
    tiles_per_seq = seq // tm
    return pl.pallas_call(
        _combine_kernel,
        grid=(t // tm,),
        in_specs=[
            pl.BlockSpec((SORT_ROWS, d), lambda i: (i, 0)),
            pl.BlockSpec((tm, LANES), lambda i: (i, 0)),
            pl.BlockSpec((tm, d), lambda i: (i, 0)),
            pl.BlockSpec((None, 1, N_ADA * d), lambda i: (i // tiles_per_seq, 0, 0)),
        ],
        out_specs=pl.BlockSpec((tm, d), lambda i: (i, 0)),
        out_shape=jax.ShapeDtypeStruct((t, d), F32),
        compiler_params=_params("parallel"),
        name="combine",
    )(ys, meta, x1, mod)


@jax.jit
def kernel(x, c, w_ada, b_ada, norm1, w_in, q_norm, k_norm, w_attn_up, w_pool, pool_scale, w_out,
           norm2, w_router_g, b_router_g, w_router_e, b_router_e, w1, w3, w2):
    bsz, seq, d = x.shape
    n_layers = w_ada.shape[0]
    t = bsz * seq
    mod_all = _ada(c, w_ada, b_ada).reshape(n_layers, bsz, 1, N_ADA * d)

    pad = LANES - N_EXPERTS - N_EXPERT_GROUPS
    w_router = jnp.concatenate(
        [w_router_e, w_router_g, jnp.zeros((n_layers, d, pad), F32)], axis=-1)
    b_router = jnp.concatenate(
        [b_router_e, b_router_g, jnp.zeros((n_layers, pad), F32)], axis=-1).reshape(n_layers, 1, LANES)
    heads_per_block = LANES // HEAD_DIM
    q_gain = jnp.tile(q_norm, (1, heads_per_block)).reshape(n_layers, 1, LANES)
    k_gain = jnp.tile(k_norm, (1, heads_per_block)).reshape(n_layers, 1, LANES)

    w1_all = w1.reshape(n_layers * N_EXPERTS, d, EXPERT_HIDDEN)
    w3_all = w3.reshape(n_layers * N_EXPERTS, d, EXPERT_HIDDEN)
    w2_all = w2.reshape(n_layers * N_EXPERTS, EXPERT_HIDDEN, d)

    xt = x.reshape(t, d)
    for l in range(n_layers):
        mod = mod_all[l]
        qkv, pin, gates = _in_proj(xt, mod, norm1[l].reshape(1, d), w_in, l, seq)
        attn = _attention(qkv, q_gain[l], k_gain[l], seq)
        x1, hs, meta, cnt = _merge(xt, attn, pin, gates, mod, w_attn_up, w_pool, pool_scale[l].reshape(1, d),
                                   w_out, norm2[l].reshape(1, d), w_router[l], b_router[l], l, seq)
        cnt = cnt[:, 0, :N_EXPERTS].astype(jnp.int32)
        ys = _experts(hs, cnt, w1_all, w3_all, w2_all, l)
        xt = _combine(ys, meta, x1, mod, seq)
    return xt.reshape(bsz, seq, d)
```

```python
import functools

import jax
import jax.numpy as jnp
from jax import lax
from jax.experimental import pallas as pl
from jax.experimental.pallas import tpu as pltpu

F32 = jnp.float32
BF16 = jnp.bfloat16

D_MODEL = 1024
N_HEADS = 8
HEAD_DIM = 64
ATTN_WIDTH = N_HEADS * HEAD_DIM
MOBA_BLOCK = 256
MOBA_TOPK = 3
POOL_WINDOWS = (2, 4, 8, 16)
N_POOL_GROUPS = len(POOL_WINDOWS)
POOL_WIDTH = D_MODEL // 2
POOL_GROUP_IN = POOL_WIDTH // N_POOL_GROUPS
POOL_GROUP_OUT = D_MODEL // N_POOL_GROUPS
N_EXPERT_GROUPS = 4
EXPERTS_PER_GROUP = 8
N_EXPERTS = N_EXPERT_GROUPS * EXPERTS_PER_GROUP
EXPERT_HIDDEN = 256
N_ADA = 6
EPS = 1e-6
LOG2_E = 1.4426950408889634
MASK_BIAS = -1e30

LANES = 128
POOL_HALO = 16
QKV_WIDTH = 3 * ATTN_WIDTH
GATE_OFF = QKV_WIDTH + POOL_WIDTH
VMEM_LIMIT = 56 * 1024 * 1024

TM_IN = 512
TM_MERGE = 512
TN_ADA = 1536
RUN_ALIGN = 16
SORT_USED = 2 * TM_MERGE + N_EXPERTS * (RUN_ALIGN - 1)
SPARE_BLOCKS = 2
SORT_ROWS = SORT_USED + SPARE_BLOCKS * RUN_ALIGN
TR_MOE = 512
BLK_PER_TILE = TR_MOE // RUN_ALIGN
GATHER_DEPTH = 4


def _params(*sem):
    return pltpu.CompilerParams(dimension_semantics=sem, vmem_limit_bytes=VMEM_LIMIT)


def _ada_kernel(c_ref, w_ref, b_ref, o_ref):
    c = c_ref[...]
    c_act = (c * jax.nn.sigmoid(c)).astype(BF16)
    o_ref[...] = jnp.dot(c_act, w_ref[...].astype(BF16), preferred_element_type=F32) + b_ref[...]


def _ada(c, w_ada, b_ada):
    n_layers, d, n = w_ada.shape
    b = c.shape[0]
    return pl.pallas_call(
        _ada_kernel,
        grid=(n_layers, n // TN_ADA),
        in_specs=[
            pl.BlockSpec((b, d), lambda l, j: (0, 0)),
            pl.BlockSpec((None, d, TN_ADA), lambda l, j: (l, 0, j)),
            pl.BlockSpec((None, 1, TN_ADA), lambda l, j: (l, 0, j)),
        ],
        out_specs=pl.BlockSpec((None, b, TN_ADA), lambda l, j: (l, 0, j)),
        out_shape=jax.ShapeDtypeStruct((n_layers, b, n), F32),
        compiler_params=_params("parallel", "parallel"),
        name="ada",
    )(c, w_ada, b_ada.reshape(n_layers, 1, n))


def _modulated_norm(x, gain, shift, scale):
    ms = jnp.mean(x * x, axis=-1, keepdims=True)
    xn = x * lax.rsqrt(ms + EPS) * gain
    return xn * (1.0 + scale) + shift


def _cast_once(src_ref, dst_ref, chunk):
    @pl.when(pl.program_id(0) == 0)
    def _():
        for c in range(0, src_ref.shape[-1], chunk):
            dst_ref[..., c:c + chunk] = src_ref[..., c:c + chunk].astype(BF16)


def _in_kernel(x_ref, mod_ref, g_ref, w32_ref, qkv_ref, pin_ref, gates_ref, w_ref):
    _cast_once(w32_ref, w_ref, POOL_WIDTH)
    h = _modulated_norm(x_ref[...], g_ref[...], mod_ref[:, 0:D_MODEL],
                        mod_ref[:, D_MODEL:2 * D_MODEL]).astype(BF16)
    qkv_ref[...] = jnp.dot(h, w_ref[:, 0:QKV_WIDTH], preferred_element_type=F32).astype(BF16)
    pin_ref[...] = jnp.dot(h, w_ref[:, QKV_WIDTH:GATE_OFF], preferred_element_type=F32).astype(BF16)
    gates_ref[...] = jnp.dot(h, w_ref[:, GATE_OFF:], preferred_element_type=F32).astype(BF16)


def _resident(shape, layer):
    zeros = (0,) * len(shape)
    return pl.BlockSpec((None,) + shape, lambda i: (layer,) + zeros, pipeline_mode=pl.Buffered(1))


def _in_proj(x, mod, gain, w_in, layer, seq):
    t, d = x.shape
    n = w_in.shape[-1]
    tiles_per_seq = seq // TM_IN
    return pl.pallas_call(
        _in_kernel,
        grid=(t // TM_IN,),
        in_specs=[
            pl.BlockSpec((TM_IN, d), lambda i: (i, 0)),
            pl.BlockSpec((None, 1, N_ADA * d), lambda i: (i // tiles_per_seq, 0, 0)),
            pl.BlockSpec((1, d), lambda i: (0, 0)),
            _resident((d, n), layer),
        ],
        out_specs=[
            pl.BlockSpec((TM_IN, QKV_WIDTH), lambda i: (i, 0)),
            pl.BlockSpec((TM_IN, POOL_WIDTH), lambda i: (i, 0)),
            pl.BlockSpec((TM_IN, 2 * d), lambda i: (i, 0)),
        ],
        out_shape=[
            jax.ShapeDtypeStruct((t, QKV_WIDTH), BF16),
            jax.ShapeDtypeStruct((t, POOL_WIDTH), BF16),
            jax.ShapeDtypeStruct((t, 2 * d), BF16),
        ],
        scratch_shapes=[pltpu.VMEM((d, n), BF16)],
        compiler_params=_params("arbitrary"),
        name="in_proj",
    )(x, mod, gain, w_in)


def _att_kernel(q_ref, k_ref, v_ref, qg_ref, kg_ref, o_ref,
                ka_ref, kb_ref, qa_ref, qb_ref, va_ref, vb_ref, km_ref, *, seq):
    nb = seq // MOBA_BLOCK
    lane = lax.broadcasted_iota(jnp.int32, (1, LANES), 1)
    is_a = lane < HEAD_DIM
    m_a = is_a.astype(F32)
    m_b = 1.0 - m_a
    nt = (((1,), (1,)), ((), ()))
    neg_inf = -jnp.inf

    def head_norm(z, gain):
        z2 = z * z
        ss_a = jnp.sum(z2 * m_a, axis=-1, keepdims=True)
        ss_b = jnp.sum(z2 * m_b, axis=-1, keepdims=True)
        r = jnp.where(is_a, lax.rsqrt(ss_a * (1.0 / HEAD_DIM) + EPS), lax.rsqrt(ss_b * (1.0 / HEAD_DIM) + EPS))
        return z * (r * gain)

    for j in range(nb):
        rows = pl.ds(j * MOBA_BLOCK, MOBA_BLOCK)
        kn = head_norm(k_ref[rows, :].astype(F32), kg_ref[...])
        km_ref[j:j + 1, :] = jnp.mean(kn, axis=0, keepdims=True)
        ka_ref[rows, :] = jnp.where(is_a, kn, (lane == HEAD_DIM + j).astype(F32)).astype(BF16)
        kb_ref[rows, :] = jnp.where(is_a, (lane == j).astype(F32), kn).astype(BF16)
        v = v_ref[rows, :]
        va_ref[rows, :] = jnp.where(is_a, v, (lane == HEAD_DIM).astype(BF16))
        vb_ref[rows, :] = jnp.where(is_a, (lane == 0).astype(BF16), v)

    km = km_ref[...]
    km_both = jnp.concatenate([km * m_a, km * m_b], axis=0)
    blk = lax.broadcasted_iota(jnp.int32, (nb, 1), 0)
    row_id = lax.broadcasted_iota(jnp.int32, (MOBA_BLOCK, MOBA_BLOCK), 0)
    col_id = lax.broadcasted_iota(jnp.int32, (MOBA_BLOCK, MOBA_BLOCK), 1)
    causal = col_id <= row_id
    eye = jnp.where(col_id == row_id, 1.0, 0.0).astype(BF16)

    qn_all = head_norm(q_ref[...].astype(F32), qg_ref[...])
    gates = lax.dot_general(km_both, qn_all, nt, precision=lax.Precision.HIGHEST,
                            preferred_element_type=F32)
    own = lax.broadcasted_iota(jnp.int32, (1, seq), 1) // MOBA_BLOCK
    bias_t = []
    for h in range(2):
        gate = jnp.where(blk < own, gates[h * nb:(h + 1) * nb, :], neg_inf)
        rank = jnp.zeros(gate.shape, jnp.int32)
        for l in range(nb - 1):
            g_l = gate[l:l + 1, :]
            ahead = (g_l > gate) | ((g_l == gate) & (l < blk))
            rank = rank + ahead.astype(jnp.int32)
        attended = ((rank < MOBA_TOPK) & (blk < own)) | (blk == own)
        bias_t.append(jnp.where(attended, 0.0, MASK_BIAS))
    pad = jnp.zeros((HEAD_DIM - nb, seq), F32)
    bias_t = jnp.concatenate([bias_t[1], pad, bias_t[0], pad], axis=0).astype(BF16)
    for i in range(nb):
        qrows = pl.ds(i * MOBA_BLOCK, MOBA_BLOCK)
        bias = lax.dot_general(eye, bias_t[:, i * MOBA_BLOCK:(i + 1) * MOBA_BLOCK], nt,
                               preferred_element_type=F32)
        q_aug = qn_all[i * MOBA_BLOCK:(i + 1) * MOBA_BLOCK] * (HEAD_DIM ** -0.5 * LOG2_E)
        qa_ref[qrows, :] = jnp.where(is_a, q_aug, bias).astype(BF16)
        qb_ref[qrows, :] = jnp.where(is_a, bias, q_aug).astype(BF16)

    for i in reversed(range(nb)):
        qrows = pl.ds(i * MOBA_BLOCK, MOBA_BLOCK)
        outs = []
        for q_ref_h, k_ref_h, v_ref_h, sum_lane in ((qa_ref, ka_ref, va_ref, HEAD_DIM),
                                                   (qb_ref, kb_ref, vb_ref, 0)):
            qh = q_ref_h[qrows, :]
            s_blocks = []
            m_run = None
            for j in range(i + 1):
                krows = pl.ds(j * MOBA_BLOCK, MOBA_BLOCK)
                s = lax.dot_general(qh, k_ref_h[krows, :], nt, preferred_element_type=F32)
                if j == i:
                    s = jnp.where(causal, s, neg_inf)
                s_blocks.append(s)
                m_blk = jnp.maximum(s[:, :LANES], s[:, LANES:])
                m_run = m_blk if m_run is None else jnp.maximum(m_run, m_blk)
            m_row = jnp.max(m_run, axis=-1, keepdims=True)
            p_blocks = [jnp.exp2(s - m_row).astype(BF16) for s in s_blocks]
            p_all = p_blocks[0] if i == 0 else jnp.concatenate(p_blocks, axis=-1)
            acc = jnp.dot(p_all, v_ref_h[0:(i + 1) * MOBA_BLOCK, :], preferred_element_type=F32)
            outs.append(acc * (1.0 / acc[:, sum_lane:sum_lane + 1]))
        o_ref[qrows, :] = jnp.where(is_a, outs[0], outs[1]).astype(o_ref.dtype)


def _attention(qkv, q_gain, k_gain, seq):
    t = qkv.shape[0]
    n_pairs = ATTN_WIDTH // LANES
    kern = functools.partial(_att_kernel, seq=seq)
    return pl.pallas_call(
        kern,
        grid=(t // seq, n_pairs),
        in_specs=[
            pl.BlockSpec((seq, LANES), lambda b, h: (b, h)),
            pl.BlockSpec((seq, LANES), lambda b, h: (b, n_pairs + h)),
            pl.BlockSpec((seq, LANES), lambda b, h: (b, 2 * n_pairs + h)),
            pl.BlockSpec((1, LANES), lambda b, h: (0, 0)),
            pl.BlockSpec((1, LANES), lambda b, h: (0, 0)),
        ],
        out_specs=pl.BlockSpec((seq, LANES), lambda b, h: (b, h)),
        out_shape=jax.ShapeDtypeStruct((t, ATTN_WIDTH), BF16),
        scratch_shapes=[
            *[pltpu.VMEM((seq, LANES), BF16)] * 6,
            pltpu.VMEM((seq // MOBA_BLOCK, LANES), F32),
        ],
        compiler_params=_params("parallel", "parallel"),
        name="moba_attention",
    )(qkv, qkv, qkv, q_gain, k_gain)


def _merge_kernel(x_ref, attn_ref, pin_ref, halo_ref, gates_ref, mod_ref, wup32_ref, wpool32_ref,
                  pscale_ref, wout32_ref, g2_ref, wr_ref, br_ref,
                  x1_ref, hs_ref, meta_ref, cnt_ref, pool_ref, wup_ref, wpool_ref, wout_ref, *, seq):
    tm = x_ref.shape[0]
    d = D_MODEL
    _cast_once(wup32_ref, wup_ref, d)
    _cast_once(wpool32_ref, wpool_ref, POOL_GROUP_OUT)
    _cast_once(wout32_ref, wout_ref, d)
    seq_pos = (pl.program_id(0) * tm) % seq

    top = 2 * POOL_HALO
    halo = halo_ref[...].astype(F32)
    pool_ref[0:POOL_HALO, :] = jnp.zeros((POOL_HALO, POOL_WIDTH), F32)
    pool_ref[POOL_HALO:top, :] = jnp.where(seq_pos == 0, 0.0, halo)
    pool_ref[top:, :] = pin_ref[...].astype(F32)
    pos = seq_pos + lax.broadcasted_iota(jnp.int32, (tm, 1), 0)

    a_up = jnp.dot(attn_ref[...], wup_ref[...], preferred_element_type=F32)
    merged = []
    for g, w in enumerate(POOL_WINDOWS):
        cols = slice(g * POOL_GROUP_IN, (g + 1) * POOL_GROUP_IN)
        tok = pool_ref[top:, cols]
        shift = 1
        while shift < w:
            pool_ref[POOL_HALO:, cols] = (pool_ref[POOL_HALO:, cols]
                                         + pool_ref[POOL_HALO - shift:top + tm - shift, cols])
            shift *= 2
        win = pool_ref[top:, cols]
        cnt = jnp.minimum(pos + 1, w).astype(F32)
        pooled = (win / cnt - tok).astype(BF16)
        ocols = slice(g * POOL_GROUP_OUT, (g + 1) * POOL_GROUP_OUT)
        b_pool = jnp.dot(pooled, wpool_ref[g], preferred_element_type=F32) * pscale_ref[:, ocols]
        ga = gates_ref[:, ocols].astype(F32)
        gp = gates_ref[:, d + g * POOL_GROUP_OUT:d + (g + 1) * POOL_GROUP_OUT].astype(F32)
        merged.append((jax.nn.sigmoid(ga) * a_up[:, ocols] + jax.nn.sigmoid(gp) * b_pool).astype(BF16))
    merged = jnp.concatenate(merged, axis=-1)
    y = jnp.dot(merged, wout_ref[...], preferred_element_type=F32)
    x1 = x_ref[...] + mod_ref[:, 2 * d:3 * d] * y
    x1_ref[...] = x1

    h2 = _modulated_norm(x1, g2_ref[...], mod_ref[:, 3 * d:4 * d], mod_ref[:, 4 * d:5 * d])

    h2_hi = h2.astype(BF16)
    h2_lo = (h2 - h2_hi.astype(F32)).astype(BF16)
    wr = wr_ref[...]
    wr_hi = wr.astype(BF16)
    wr_lo = (wr - wr_hi.astype(F32)).astype(BF16)
    logits = (jnp.dot(h2_hi, wr_hi, preferred_element_type=F32)
              + jnp.dot(h2_lo, wr_hi, preferred_element_type=F32)
              + jnp.dot(h2_hi, wr_lo, preferred_element_type=F32)) + br_ref[...]
    lane = lax.broadcasted_iota(jnp.int32, (1, LANES), 1)
    neg_inf = -jnp.inf
    is_group = (lane >= N_EXPERTS) & (lane < N_EXPERTS + N_EXPERT_GROUPS)
    gl = jnp.where(is_group, logits, neg_inf)
    g_max = jnp.max(gl, axis=-1, keepdims=True)
    g_top = 1.0 / jnp.sum(jnp.exp(gl - g_max), axis=-1, keepdims=True)
    g_idx = jnp.min(jnp.where(gl == g_max, lane, LANES), axis=-1, keepdims=True) - N_EXPERTS
    in_group = (lane < N_EXPERTS) & ((lane // EXPERTS_PER_GROUP) == g_idx)
    el = jnp.where(in_group, logits, neg_inf)
    e1 = jnp.max(el, axis=-1, keepdims=True)
    i1 = jnp.min(jnp.where(el == e1, lane, LANES), axis=-1, keepdims=True)
    el2 = jnp.where(lane == i1, neg_inf, el)
    e2 = jnp.max(el2, axis=-1, keepdims=True)
    i2 = jnp.min(jnp.where(el2 == e2, lane, LANES), axis=-1, keepdims=True)
    r = jnp.exp(e2 - e1)
    w_first = g_top / (1.0 + r)
    w_second = w_first * r

    onehot1 = jnp.where(lane == i1, 1.0, 0.0)
    onehot2 = jnp.where(lane == i2, 1.0, 0.0)
    t_row = lax.broadcasted_iota(jnp.int32, (tm, tm), 0)
    t_col = lax.broadcasted_iota(jnp.int32, (tm, tm), 1)
    earlier = jnp.where(t_col < t_row, 1.0, 0.0).astype(BF16)
    both = jnp.concatenate([onehot1, onehot2], axis=-1).astype(BF16)
    before = jnp.dot(earlier, both, preferred_element_type=F32)
    before1, before2 = before[:, :LANES], before[:, LANES:]
    cnt1 = jnp.sum(onehot1, axis=0, keepdims=True)
    cnt = cnt1 + jnp.sum(onehot2, axis=0, keepdims=True)
    run_len = ((cnt.astype(jnp.int32) + (RUN_ALIGN - 1)) // RUN_ALIGN * RUN_ALIGN).astype(F32)
    e_row = lax.broadcasted_iota(jnp.int32, (LANES, LANES), 0)
    e_col = lax.broadcasted_iota(jnp.int32, (LANES, LANES), 1)
    lower_experts = jnp.where(e_row < e_col, 1.0, 0.0)
    run_start = jnp.dot(jnp.broadcast_to(run_len, (8, LANES)), lower_experts,
                        precision=lax.Precision.HIGHEST, preferred_element_type=F32)[0:1, :]
    pos1 = jnp.sum(onehot1 * (run_start + before1), axis=-1, keepdims=True)
    pos2 = jnp.sum(onehot2 * (run_start + cnt1 + before2), axis=-1, keepdims=True)
    meta = jnp.where(lane == 0, pos1, jnp.where(lane == 1, pos2,
                     jnp.where(lane == 2, w_first, jnp.where(lane == 3, w_second, 0.0))))
    meta_ref[...] = meta
    cnt_ref[...] = jnp.broadcast_to(cnt, (8, LANES))

    eye8 = jnp.where(lax.broadcasted_iota(jnp.int32, (8, LANES), 0)
                     == lax.broadcasted_iota(jnp.int32, (8, LANES), 1), 1.0, 0.0)
    pos_rows = lax.dot_general(eye8, meta, (((1,), (1,)), ((), ())),
                               precision=lax.Precision.HIGHEST, preferred_element_type=F32)
    r_id = lax.broadcasted_iota(jnp.int32, (SORT_ROWS, tm), 0).astype(F32)
    hit = (r_id == pos_rows[0:1, :]) | (r_id == pos_rows[1:2, :])
    perm = jnp.where(hit, 1.0, 0.0).astype(BF16)
    hs_ref[...] = jnp.dot(perm, h2_hi, preferred_element_type=F32).astype(BF16)


def _merge(x, attn, pin, gates, mod, w_up, w_pool, pool_scale, w_out, gain2, w_router, b_router, layer, seq):
    t, d = x.shape
    tm = TM_MERGE
    n_tiles = t // tm
    tiles_per_seq = seq // tm
    halo_per_tile = tm // POOL_HALO
    kern = functools.partial(_merge_kernel, seq=seq)
    const2 = lambda i: (0, 0)
    return pl.pallas_call(
        kern,
        grid=(t // tm,),
        in_specs=[
            pl.BlockSpec((tm, d), lambda i: (i, 0)),
            pl.BlockSpec((tm, ATTN_WIDTH), lambda i: (i, 0)),
            pl.BlockSpec((tm, POOL_WIDTH), lambda i: (i, 0)),
            pl.BlockSpec((POOL_HALO, POOL_WIDTH), lambda i: (jnp.maximum(i * halo_per_tile - 1, 0), 0)),
            pl.BlockSpec((tm, 2 * d), lambda i: (i, 0)),
            pl.BlockSpec((None, 1, N_ADA * d), lambda i: (i // tiles_per_seq, 0, 0)),
            _resident((ATTN_WIDTH, d), layer),
            _resident((N_POOL_GROUPS, POOL_GROUP_IN, POOL_GROUP_OUT), layer),
            pl.BlockSpec((1, d), const2),
            _resident((d, d), layer),
            pl.BlockSpec((1, d), const2),
            pl.BlockSpec((d, LANES), const2),
            pl.BlockSpec((1, LANES), const2),
        ],
        out_specs=[
            pl.BlockSpec((tm, d), lambda i: (i, 0)),
            pl.BlockSpec((SORT_ROWS, d), lambda i: (i, 0)),
            pl.BlockSpec((tm, LANES), lambda i: (i, 0)),
            pl.BlockSpec((None, 8, LANES), lambda i: (i, 0, 0)),
        ],
        out_shape=[
            jax.ShapeDtypeStruct((t, d), F32),
            jax.ShapeDtypeStruct((n_tiles * SORT_ROWS, d), BF16),
            jax.ShapeDtypeStruct((t, LANES), F32),
            jax.ShapeDtypeStruct((n_tiles, 8, LANES), F32),
        ],
        scratch_shapes=[
            pltpu.VMEM((tm + 2 * POOL_HALO, POOL_WIDTH), F32),
            pltpu.VMEM((ATTN_WIDTH, d), BF16),
            pltpu.VMEM((N_POOL_GROUPS, POOL_GROUP_IN, POOL_GROUP_OUT), BF16),
            pltpu.VMEM((d, d), BF16),
        ],
        compiler_params=_params("arbitrary"),
        name="merge_router",
    )(x, attn, pin, pin, gates, mod, w_up, w_pool, pool_scale, w_out, gain2, w_router, b_router)


def _dispatch_plan(cnt, n_tiles_max):
    n_merge = cnt.shape[0]
    i32 = jnp.int32

    def before(n):
        return (jnp.arange(n, dtype=i32)[:, None] < jnp.arange(n, dtype=i32)[None, :]).astype(i32)

    run_len = (cnt + (RUN_ALIGN - 1)) // RUN_ALIGN * RUN_ALIGN
    run_start = jnp.sum(run_len[:, :, None] * before(N_EXPERTS)[None], axis=1)
    run_blocks = (run_len // RUN_ALIGN).T
    run_first = jnp.sum(run_blocks[:, :, None] * before(n_merge)[None], axis=1)
    blocks_e = jnp.sum(run_blocks, axis=1)
    tiles_e = (blocks_e + BLK_PER_TILE - 1) // BLK_PER_TILE
    tiles_first = jnp.sum(tiles_e[:, None] * before(N_EXPERTS), axis=0)
    n_used = jnp.sum(tiles_e)
    tile_id = jnp.arange(n_tiles_max, dtype=i32)
    tile_e = jnp.sum(((tiles_first + tiles_e)[None, :] <= tile_id[:, None]).astype(i32), axis=1)
    tile_e = jnp.minimum(tile_e, N_EXPERTS - 1)
    of_tile = (tile_e[:, None] == jnp.arange(N_EXPERTS, dtype=i32)[None, :]).astype(i32)

    def per_tile(table):
        return jnp.sum(of_tile[:, :, None] * table[None], axis=1)

    tile_first = jnp.sum(of_tile * tiles_first[None, :], axis=1)
    tile_blocks = jnp.sum(of_tile * blocks_e[None, :], axis=1)
    first_t, nblk_t, start_t = per_tile(run_first), per_tile(run_blocks), per_tile(run_start.T)

    slot = jnp.arange(BLK_PER_TILE, dtype=i32)[None, :]
    blk = (tile_id - tile_first)[:, None] * BLK_PER_TILE + slot
    valid = (tile_id[:, None] < n_used) & (blk < tile_blocks[:, None])
    off = blk[:, :, None] - first_t[:, None, :]
    in_run = (off >= 0) & (off < nblk_t[:, None, :])
    row = jnp.arange(n_merge, dtype=i32) * SORT_ROWS + start_t[:, None, :] + off * RUN_ALIGN
    src = jnp.sum(jnp.where(in_run, row, 0), axis=-1)
    assert n_merge * SPARE_BLOCKS >= 2 * BLK_PER_TILE
    k = (tile_id[:, None] % 2) * BLK_PER_TILE + slot
    trash = (k // SPARE_BLOCKS) * SORT_ROWS + SORT_USED + (k % SPARE_BLOCKS) * RUN_ALIGN
    dst = jnp.where(valid, src, trash).astype(jnp.int32).reshape(-1)
    src = jnp.where(valid, src, jnp.where(valid[:, 0:1], src[:, 0:1], 0)).astype(jnp.int32).reshape(-1)
    return tile_e, n_used.astype(jnp.int32).reshape(1), src, dst


def _expert_kernel(tile_e_ref, n_used_ref, src_ref, dst_ref,
                   hs_ref, w1_ref, w3_ref, w2_ref, ys_ref,
                   xbuf, ybuf, w1b, w3b, w2b, gsem, ssem):
    i = pl.program_id(0)
    n_used = n_used_ref[0]
    slot = i % 2

    def gather_copy(tile, s, m):
        src = pl.multiple_of(src_ref[tile * BLK_PER_TILE + m], RUN_ALIGN)
        return pltpu.make_async_copy(hs_ref.at[pl.ds(src, RUN_ALIGN), :],
                                     xbuf.at[s, pl.ds(m * RUN_ALIGN, RUN_ALIGN), :], gsem.at[s])

    def scatter_copy(tile, s, m):
        dst = pl.multiple_of(dst_ref[tile * BLK_PER_TILE + m], RUN_ALIGN)
        return pltpu.make_async_copy(ybuf.at[s, pl.ds(m * RUN_ALIGN, RUN_ALIGN), :],
                                     ys_ref.at[pl.ds(dst, RUN_ALIGN), :], ssem.at[s])

    def start_all(copy, tile, s):
        for m in range(BLK_PER_TILE):
            copy(tile, s, m).start()

    def wait_all(copy, tile, s):
        del tile
        if copy is gather_copy:
            pltpu.make_async_copy(hs_ref.at[pl.ds(0, TR_MOE), :], xbuf.at[s], gsem.at[s]).wait()
        else:
            pltpu.make_async_copy(ybuf.at[s], ys_ref.at[pl.ds(0, TR_MOE), :], ssem.at[s]).wait()

    xslot = i % GATHER_DEPTH

    @pl.when(i == 0)
    def _():
        for t in range(GATHER_DEPTH - 1):
            @pl.when(t < n_used)
            def _():
                start_all(gather_copy, t, t)

    @pl.when(i + (GATHER_DEPTH - 1) < n_used)
    def _():
        start_all(gather_copy, i + (GATHER_DEPTH - 1), (i + (GATHER_DEPTH - 1)) % GATHER_DEPTH)

    @pl.when(i < n_used)
    def _():
        e = tile_e_ref[i]
        e_prev = tile_e_ref[jnp.maximum(i - 1, 0)]

        @pl.when((i == 0) | (e != e_prev))
        def _():
            w1b[...] = w1_ref[...].astype(BF16)
            w3b[...] = w3_ref[...].astype(BF16)
            w2b[...] = w2_ref[...].astype(BF16)

        wait_all(gather_copy, i, xslot)

        @pl.when(i >= 2)
        def _():
            wait_all(scatter_copy, i - 2, slot)

        x = xbuf[xslot]
        a = jnp.dot(x, w1b[...], preferred_element_type=F32)
        b = jnp.dot(x, w3b[...], preferred_element_type=F32)
        hid = (a * jax.nn.sigmoid(a) * b).astype(BF16)
        ybuf[slot] = jnp.dot(hid, w2b[...], preferred_element_type=F32).astype(BF16)
        start_all(scatter_copy, i, slot)

        @pl.when(i == n_used - 1)
        def _():
            @pl.when(i >= 1)
            def _():
                wait_all(scatter_copy, i - 1, 1 - slot)
            wait_all(scatter_copy, i, slot)


def _experts(hs, cnt, w1, w3, w2, layer):
    n_rows, d = hs.shape
    f = w1.shape[-1]
    first_expert = layer * N_EXPERTS
    n_tiles_max = n_rows // TR_MOE + N_EXPERTS
    tile_e, n_used, src, dst = _dispatch_plan(cnt, n_tiles_max)
    w_in_map = lambda i, te, nu, s, dd: (first_expert + te[i], 0, 0)
    grid_spec = pltpu.PrefetchScalarGridSpec(
        num_scalar_prefetch=4,
        grid=(n_tiles_max,),
        in_specs=[
            pl.BlockSpec(memory_space=pl.ANY),
            pl.BlockSpec((None, d, f), w_in_map),
            pl.BlockSpec((None, d, f), w_in_map),
            pl.BlockSpec((None, f, d), w_in_map),
        ],
        out_specs=pl.BlockSpec(memory_space=pl.ANY),
        scratch_shapes=[
            pltpu.VMEM((GATHER_DEPTH, TR_MOE, d), BF16),
            pltpu.VMEM((2, TR_MOE, d), BF16),
            pltpu.VMEM((d, f), BF16),
            pltpu.VMEM((d, f), BF16),
            pltpu.VMEM((f, d), BF16),
            pltpu.SemaphoreType.DMA((GATHER_DEPTH,)),
            pltpu.SemaphoreType.DMA((2,)),
        ],
    )
    return pl.pallas_call(
        _expert_kernel,
        grid_spec=grid_spec,
        out_shape=jax.ShapeDtypeStruct(hs.shape, hs.dtype),
        input_output_aliases={4: 0},
        compiler_params=_params("arbitrary"),
        name="experts",
    )(tile_e, n_used, src, dst, hs, w1, w3, w2)


def _combine_kernel(ys_ref, meta_ref, x1_ref, mod_ref, o_ref):
    tm = x1_ref.shape[0]
    d = D_MODEL
    meta = meta_ref[...]
    r_id = lax.broadcasted_iota(jnp.int32, (tm, SORT_ROWS), 1).astype(F32)
    ys = ys_ref[...]
    pick1 = jnp.where(r_id == meta[:, 0:1], 1.0, 0.0).astype(BF16)
    pick2 = jnp.where(r_id == meta[:, 1:2], 1.0, 0.0).astype(BF16)
    y = (meta[:, 2:3] * jnp.dot(pick1, ys, preferred_element_type=F32)
         + meta[:, 3:4] * jnp.dot(pick2, ys, preferred_element_type=F32))
    o_ref[...] = x1_ref[...] + mod_ref[:, 5 * d:6 * d] * y


def _combine(ys, meta, x1, mod, seq):
    t, d = x1.shape
    tm = TM_MERGE
    tiles_per_seq = seq // tm
    return pl.pallas_call(
        _combine_kernel,
        grid=(t // tm,),
        in_specs=[
            pl.BlockSpec((SORT_ROWS, d), lambda i: (i, 0)),
            pl.BlockSpec((tm, LANES), lambda i: (i, 0)),
            pl.BlockSpec((tm, d), lambda i: (i, 0)),
            pl.BlockSpec((None, 1, N_ADA * d), lambda i: (i // tiles_per_seq, 0, 0)),
        ],
        out_specs=pl.BlockSpec((tm, d), lambda i: (i, 0)),
        out_shape=jax.ShapeDtypeStruct((t, d), F32),
        compiler_params=_params("parallel"),
        name="combine",
    )(ys, meta, x1, mod)


@jax.jit
def kernel(x, c, w_ada, b_ada, norm1, w_in, q_norm, k_norm, w_attn_up, w_pool, pool_scale, w_out,
           norm2, w_router_g, b_router_g, w_router_e, b_router_e, w1, w3, w2):
    bsz, seq, d = x.shape
    n_layers = w_ada.shape[0]
    t = bsz * seq
    mod_all = _ada(c, w_ada, b_ada).reshape(n_layers, bsz, 1, N_ADA * d)

    pad = LANES - N_EXPERTS - N_EXPERT_GROUPS
    w_router = jnp.concatenate(
        [w_router_e, w_router_g, jnp.zeros((n_layers, d, pad), F32)], axis=-1)
    b_router = jnp.concatenate(
        [b_router_e, b_router_g, jnp.zeros((n_layers, pad), F32)], axis=-1).reshape(n_layers, 1, LANES)
    heads_per_block = LANES // HEAD_DIM
    q_gain = jnp.tile(q_norm, (1, heads_per_block)).reshape(n_layers, 1, LANES)
    k_gain = jnp.tile(k_norm, (1, heads_per_block)).reshape(n_layers, 1, LANES)

    w1_all = w1.reshape(n_layers * N_EXPERTS, d, EXPERT_HIDDEN)
    w3_all = w3.reshape(n_layers * N_EXPERTS, d, EXPERT_HIDDEN)
    w2_all = w2.reshape(n_layers * N_EXPERTS, EXPERT_HIDDEN, d)

    xt = x.reshape(t, d)
    for l in range(n_layers):
        mod = mod_all[l]
        qkv, pin, gates = _in_proj(xt, mod, norm1[l].reshape(1, d), w_in, l, seq)
        attn = _attention(qkv, q_gain[l], k_gain[l], seq)
        x1, hs, meta, cnt = _merge(xt, attn, pin, gates, mod, w_attn_up, w_pool, pool_scale[l].reshape(1, d),
                                   w_out, norm2[l].reshape(1, d), w_router[l], b_router[l], l, seq)
        cnt = cnt[:, 0, :N_EXPERTS].astype(jnp.int32)
        ys = _experts(hs, cnt, w1_all, w3_all, w2_all, l)
        xt = _combine(ys, meta, x1, mod, seq)
    return xt.reshape(bsz, seq, d)
```
